```python
import math
import jax
import jax.numpy as jnp
from jax import lax
import numpy as np

D_MODEL = 2048
BATCH = 8
SEQ = 2048
DEPTH = 2

N_MIXERS = 2
N_ATTN_LAYERS = (DEPTH + 1) // 2
N_RET_LAYERS = DEPTH // 2

GRID_W = 64

NA_HEADS = 16
NA_HEAD_DIM = D_MODEL // NA_HEADS
NA_MAX_KH = 8
NA_KW = 16

RET_HEADS = 8
RET_QK_DIM = D_MODEL // RET_HEADS
RET_V_DIM = 2 * RET_QK_DIM
RET_QK_WIDTH = RET_HEADS * RET_QK_DIM
RET_V_WIDTH = RET_HEADS * RET_V_DIM
RET_CHUNK = 128
ROPE_BASE = 10000.0

N_GROUPS = 4
EXPERTS_PER_GROUP = 8
TOP_K = 2
EXPERT_FF = D_MODEL // 4

DEEPNORM_ALPHA = (2 * DEPTH) ** 0.25
DEEPNORM_BETA = (8 * DEPTH) ** -0.25
LN_EPS = 1e-5
GN_EPS = 1e-6

kernel_name = "hybrid_natten_retention_hmoe_deepnorm"


def layer_norm(x, gain, bias):
    xf = x.astype(jnp.float32)
    mu = jnp.mean(xf, axis=-1, keepdims=True)
    var = jnp.mean(jnp.square(xf - mu), axis=-1, keepdims=True)
    y = (xf - mu) * lax.rsqrt(var + LN_EPS)
    return (y * gain.astype(jnp.float32) + bias.astype(jnp.float32)).astype(x.dtype)


def neighbourhood_attention(x, w_in, w_out, rpb):
    B, S, D = x.shape
    rows = S // GRID_W
    kh = min(NA_MAX_KH, rows)
    qkv = x @ w_in
    q, k, v = jnp.split(qkv, 3, axis=-1)
    grid = lambda t: t.reshape(B, rows, GRID_W, NA_HEADS, NA_HEAD_DIM)
    q = grid(q) * (NA_HEAD_DIM ** -0.5)
    k = grid(k)
    v = grid(v)
    cols = jnp.arange(GRID_W)
    col_start = jnp.clip(cols - NA_KW // 2, 0, GRID_W - NA_KW)
    col_in_win = (cols[None, :] >= col_start[:, None]) & (cols[None, :] < col_start[:, None] + NA_KW)
    dc = jnp.clip(cols[None, :] - cols[:, None], -(NA_KW - 1), NA_KW - 1) + NA_KW - 1
    rpb_c = jnp.transpose(rpb[:, :, dc], (0, 2, 1, 3))

    def row_block(r):
        rs = jnp.clip(r - kh // 2, 0, rows - kh)
        k_blk = lax.dynamic_slice_in_dim(k, rs, kh, axis=1)
        v_blk = lax.dynamic_slice_in_dim(v, rs, kh, axis=1)
        q_row = lax.dynamic_index_in_dim(q, r, axis=1, keepdims=False)
        dr = rs + jnp.arange(kh) - r + NA_MAX_KH - 1
        bias = jnp.take(rpb_c, dr, axis=2)
        s = jnp.einsum('bqhd,brkhd->bhqrk', q_row, k_blk).astype(jnp.float32)
        s = s + bias.astype(jnp.float32)[None]
        s = jnp.where(col_in_win[None, None, :, None, :], s, -jnp.inf)
        p = jax.nn.softmax(s.reshape(B, NA_HEADS, GRID_W, kh * GRID_W), axis=-1)
        p = p.reshape(B, NA_HEADS, GRID_W, kh, GRID_W).astype(v.dtype)
        return jnp.einsum('bhqrk,brkhd->bqhd', p, v_blk)

    out = lax.map(row_block, jnp.arange(rows))
    out = jnp.transpose(out, (1, 0, 2, 3, 4)).reshape(B, S, D)
    return out @ w_out


def rotary(t, positions):
    half = t.shape[-1] // 2
    freqs = ROPE_BASE ** (-jnp.arange(half, dtype=jnp.float32) / half)
    ang = positions.astype(jnp.float32)[:, None] * freqs[None, :]
    cos = jnp.cos(ang)[None, :, None, :]
    sin = jnp.sin(ang)[None, :, None, :]
    t1 = t[..., :half].astype(jnp.float32)
    t2 = t[..., half:].astype(jnp.float32)
    return jnp.concatenate([t1 * cos - t2 * sin, t1 * sin + t2 * cos], axis=-1).astype(t.dtype)


def chunk_retention(q, k, v, log_gamma, strict):
    B, H, S, dk = q.shape
    dv = v.shape[-1]
    C = RET_CHUNK
    n_chunks = S // C
    chunks = lambda t: jnp.transpose(t.reshape(B, H, n_chunks, C, t.shape[-1]), (2, 0, 1, 3, 4)).astype(jnp.float32)
    qc, kc, vc = chunks(q), chunks(k), chunks(v)
    lg = log_gamma.astype(jnp.float32)
    idx = jnp.arange(C, dtype=jnp.float32)
    diff = idx[:, None] - idx[None, :]
    allowed = (diff > 0) if strict else (diff >= 0)
    decay_mask = jnp.where(allowed[None], jnp.exp(lg[:, None, None] * jnp.maximum(diff, 0.0)[None]), 0.0)
    q_decay = jnp.exp(lg[:, None] * (idx[None, :] + 1.0))
    k_decay = jnp.exp(lg[:, None] * (C - 1.0 - idx[None, :]))
    chunk_decay = jnp.exp(lg * C)

    def step(state, inp):
        qi, ki, vi = inp
        inner = jnp.einsum('bhid,bhjd->bhij', qi, ki) * decay_mask[None]
        inner_out = jnp.einsum('bhij,bhjv->bhiv', inner, vi)
        cross = jnp.einsum('bhid,bhdv->bhiv', qi, state) * q_decay[None, :, :, None]
        new_state = state * chunk_decay[None, :, None, None] + jnp.einsum(
            'bhjd,bhjv->bhdv', ki * k_decay[None, :, :, None], vi)
        return new_state, inner_out + cross

    state0 = jnp.zeros((B, H, dk, dv), jnp.float32)
    _, out = lax.scan(step, state0, (qc, kc, vc))
    return jnp.transpose(out, (1, 2, 0, 3, 4)).reshape(B, H, S, dv)


def retention_mixer(x, w_in, w_out, decay_fwd, decay_bwd):
    B, S, D = x.shape
    proj = x @ w_in
    q, k, v, g = jnp.split(proj, [RET_QK_WIDTH, 2 * RET_QK_WIDTH, 2 * RET_QK_WIDTH + RET_V_WIDTH], axis=-1)
    pos = jnp.arange(S)
    q = rotary(q.reshape(B, S, RET_HEADS, RET_QK_DIM), pos)
    k = rotary(k.reshape(B, S, RET_HEADS, RET_QK_DIM), pos) * (RET_QK_DIM ** -0.5)
    v = v.reshape(B, S, RET_HEADS, RET_V_DIM)
    q, k, v = (jnp.transpose(t, (0, 2, 1, 3)) for t in (q, k, v))
    lg_f = jnp.log1p(-jnp.exp(decay_fwd.astype(jnp.float32)))
    lg_b = jnp.log1p(-jnp.exp(decay_bwd.astype(jnp.float32)))
    y_fwd = chunk_retention(q, k, v, lg_f, strict=False)
    flip = lambda t: jnp.flip(t, axis=2)
    y_bwd = flip(chunk_retention(flip(q), flip(k), flip(v), lg_b, strict=True))
    y = y_fwd + y_bwd
    y = y * lax.rsqrt(jnp.mean(jnp.square(y), axis=-1, keepdims=True) + GN_EPS)
    y = jnp.transpose(y, (0, 2, 1, 3)).reshape(B, S, RET_V_WIDTH).astype(x.dtype)
    return (jax.nn.silu(g) * y) @ w_out


def hierarchical_moe(x, w_group_router, b_group_router, w_expert_router, b_expert_router, w_gate, w_up, w_down):
    B, S, D = x.shape
    xt = x.reshape(B * S, D)
    n_tok = xt.shape[0]
    group_logits = (xt @ w_group_router).astype(jnp.float32) + b_group_router.astype(jnp.float32)
    group_probs = jax.nn.softmax(group_logits, axis=-1)
    g_sel = jnp.argmax(group_logits, axis=-1)
    w_grp = jnp.take_along_axis(group_probs, g_sel[:, None], axis=-1)
    exp_logits = (xt @ w_expert_router.reshape(D, N_GROUPS * EXPERTS_PER_GROUP)).astype(jnp.float32)
    exp_logits = exp_logits.reshape(n_tok, N_GROUPS, EXPERTS_PER_GROUP) + b_expert_router.astype(jnp.float32)
    sel_logits = jnp.take_along_axis(exp_logits, g_sel[:, None, None], axis=1)[:, 0]
    top_vals, top_idx = lax.top_k(sel_logits, TOP_K)
    top_w = jax.nn.softmax(top_vals, axis=-1) * w_grp
    within = jnp.sum(jax.nn.one_hot(top_idx, EXPERTS_PER_GROUP, dtype=jnp.float32) * top_w[..., None], axis=1)
    out = jnp.zeros_like(xt)
    for grp in range(N_GROUPS):
        comb = jnp.where((g_sel == grp)[:, None], within, 0.0).astype(x.dtype)
        h = jax.nn.silu(jnp.einsum('nd,edf->nef', xt, w_gate[grp])) * jnp.einsum('nd,edf->nef', xt, w_up[grp])
        out = out + jnp.einsum('nef,efd->nd', h * comb[:, :, None], w_down[grp])
    return out.reshape(B, S, D)


def _normal(key, shape, scale):
    return jax.random.normal(key, shape, jnp.float32) * scale


def setup_inputs(seed: int = 0) -> dict:
    key = jax.random.key(seed)
    ks = jax.random.split(key, 24)
    D = D_MODEL
    x = jax.random.normal(ks[0], (BATCH, SEQ, D), jnp.float32)
    attn_w_in = jnp.concatenate([
        _normal(ks[1], (N_ATTN_LAYERS, D, 2 * D), D ** -0.5),
        _normal(ks[2], (N_ATTN_LAYERS, D, D), DEEPNORM_BETA * D ** -0.5)], axis=-1)
    attn_w_out = _normal(ks[3], (N_ATTN_LAYERS, D, D), DEEPNORM_BETA * D ** -0.5)
    attn_rpb = _normal(ks[4], (N_ATTN_LAYERS, NA_HEADS, 2 * NA_MAX_KH - 1, 2 * NA_KW - 1), 0.1)
    ret_w_in = jnp.concatenate([
        _normal(ks[5], (N_RET_LAYERS, D, 2 * RET_QK_WIDTH), D ** -0.5),
        _normal(ks[6], (N_RET_LAYERS, D, RET_V_WIDTH), DEEPNORM_BETA * D ** -0.5),
        _normal(ks[7], (N_RET_LAYERS, D, RET_V_WIDTH), D ** -0.5)], axis=-1)
    ret_w_out = _normal(ks[8], (N_RET_LAYERS, RET_V_WIDTH, D), DEEPNORM_BETA * RET_V_WIDTH ** -0.5)
    base = -math.log(2.0) * (5.0 + jnp.arange(RET_HEADS, dtype=jnp.float32))
    ret_decay_fwd = base[None] + _normal(ks[9], (N_RET_LAYERS, RET_HEADS), 0.1)
    ret_decay_bwd = base[None] + _normal(ks[10], (N_RET_LAYERS, RET_HEADS), 0.1)
    moe_w_group_router = _normal(ks[11], (DEPTH, D, N_GROUPS), D ** -0.5)
    moe_b_group_router = _normal(ks[12], (DEPTH, N_GROUPS), 0.01)
    moe_w_expert_router = _normal(ks[13], (DEPTH, D, N_GROUPS, EXPERTS_PER_GROUP), D ** -0.5)
    moe_b_expert_router = _normal(ks[14], (DEPTH, N_GROUPS, EXPERTS_PER_GROUP), 0.01)
    moe_w_gate = _normal(ks[15], (DEPTH, N_GROUPS, EXPERTS_PER_GROUP, D, EXPERT_FF), D ** -0.5)
    moe_w_up = _normal(ks[16], (DEPTH, N_GROUPS, EXPERTS_PER_GROUP, D, EXPERT_FF), D ** -0.5)
    moe_w_down = _normal(ks[17], (DEPTH, N_GROUPS, EXPERTS_PER_GROUP, EXPERT_FF, D), DEEPNORM_BETA * EXPERT_FF ** -0.5)
    ln_gain = 1.0 + _normal(ks[18], (DEPTH, 2, D), 0.02)
    ln_bias = _normal(ks[19], (DEPTH, 2, D), 0.02)
    return {"x": x, "attn_w_in": attn_w_in, "attn_w_out": attn_w_out, "attn_rpb": attn_rpb,
            "ret_w_in": ret_w_in, "ret_w_out": ret_w_out, "ret_decay_fwd": ret_decay_fwd,
            "ret_decay_bwd": ret_decay_bwd, "moe_w_group_router": moe_w_group_router,
            "moe_b_group_router": moe_b_group_router, "moe_w_expert_router": moe_w_expert_router,
            "moe_b_expert_router": moe_b_expert_router, "moe_w_gate": moe_w_gate, "moe_w_up": moe_w_up,
            "moe_w_down": moe_w_down, "ln_gain": ln_gain, "ln_bias": ln_bias}


def reference(x, attn_w_in, attn_w_out, attn_rpb, ret_w_in, ret_w_out, ret_decay_fwd, ret_decay_bwd,
              moe_w_group_router, moe_b_group_router, moe_w_expert_router, moe_b_expert_router,
              moe_w_gate, moe_w_up, moe_w_down, ln_gain, ln_bias):
    for i in range(DEPTH):
        j = i // N_MIXERS
        if i % N_MIXERS == 0:
            mix = neighbourhood_attention(x, attn_w_in[j], attn_w_out[j], attn_rpb[j])
        else:
            mix = retention_mixer(x, ret_w_in[j], ret_w_out[j], ret_decay_fwd[j], ret_decay_bwd[j])
        x = layer_norm(DEEPNORM_ALPHA * x + mix, ln_gain[i, 0], ln_bias[i, 0])
        ffn = hierarchical_moe(x, moe_w_group_router[i], moe_b_group_router[i], moe_w_expert_router[i],
                               moe_b_expert_router[i], moe_w_gate[i], moe_w_up[i], moe_w_down[i])
        x = layer_norm(DEEPNORM_ALPHA * x + ffn, ln_gain[i, 1], ln_bias[i, 1])
    return x
```

```python
import functools
import math

import jax
import jax.numpy as jnp
from jax import lax
from jax.experimental import pallas as pl
from jax.experimental.pallas import tpu as pltpu

GRID_W = 64
NA_HEADS = 16
NA_HEAD_DIM = 128
NA_MAX_KH = 8
NA_KW = 16
RET_HEADS = 8
RET_QK_DIM = 256
RET_V_DIM = 512
ROPE_BASE = 10000.0
N_GROUPS = 4
EXPERTS_PER_GROUP = 8
N_EXPERTS = N_GROUPS * EXPERTS_PER_GROUP
TOP_K = 2
DEPTH = 2
DEEPNORM_ALPHA = (2 * DEPTH) ** 0.25
LN_EPS = 1e-5
GN_EPS = 1e-6

V7X_LANES = 128
V7X_VMEM_LIMIT_BYTES = 56 * 1024 * 1024

BF16 = jnp.bfloat16
F32 = jnp.float32

MM_TM = 1024
MM_TN = 1024
LN_TM = 512
ROUTER_TM = 512
MOE_TM = 512
COMBINE_TM = 256
RET_CHUNK = 256
ROUTER_LANES = V7X_LANES


def _params(semantics):
    return pltpu.CompilerParams(dimension_semantics=semantics, vmem_limit_bytes=V7X_VMEM_LIMIT_BYTES)


def _matmul_kernel(x_ref, w_ref, o_ref):
    o_ref[...] = jnp.dot(x_ref[...], w_ref[...], preferred_element_type=F32).astype(o_ref.dtype)


def matmul(x, w, *, tm=MM_TM, tn=MM_TN):
    m, k = x.shape
    n = w.shape[1]
    tm, tn = min(tm, m), min(tn, n)
    assert m % tm == 0 and n % tn == 0
    return pl.pallas_call(
        _matmul_kernel,
        grid=(m // tm, n // tn),
        in_specs=[pl.BlockSpec((tm, k), lambda i, j: (i, 0)),
                  pl.BlockSpec((k, tn), lambda i, j: (0, j))],
        out_specs=pl.BlockSpec((tm, tn), lambda i, j: (i, j)),
        out_shape=jax.ShapeDtypeStruct((m, n), BF16),
        compiler_params=_params(("parallel", "arbitrary")),
        name="proj_matmul",
    )(x, w)


def _layer_norm_rows(z, gain, bias):
    mu = jnp.mean(z, axis=-1, keepdims=True)
    zc = z - mu
    var = jnp.mean(zc * zc, axis=-1, keepdims=True)
    return zc * lax.rsqrt(var + LN_EPS) * gain + bias


def _mm_res_ln_kernel(a_ref, w_ref, x_ref, g_ref, b_ref, o32_ref, obf_ref, acc_ref, *, nk):
    k = pl.program_id(1)

    @pl.when(k == 0)
    def _():
        acc_ref[...] = jnp.zeros_like(acc_ref)

    acc_ref[...] += jnp.dot(a_ref[...], w_ref[...], preferred_element_type=F32)

    @pl.when(k == nk - 1)
    def _():
        y = _layer_norm_rows(DEEPNORM_ALPHA * x_ref[...] + acc_ref[...], g_ref[...], b_ref[...])
        o32_ref[...] = y
        obf_ref[...] = y.astype(BF16)


def matmul_residual_layernorm(a, w, x, gain, bias, *, tm=LN_TM, tk=2048):
    m, kdim = a.shape
    d = w.shape[1]
    tm = min(tm, m)
    assert m % tm == 0 and kdim % tk == 0
    nk = kdim // tk
    return pl.pallas_call(
        functools.partial(_mm_res_ln_kernel, nk=nk),
        grid=(m // tm, nk),
        in_specs=[pl.BlockSpec((tm, tk), lambda i, k: (i, k)),
                  pl.BlockSpec((tk, d), lambda i, k: (k, 0)),
                  pl.BlockSpec((tm, d), lambda i, k: (i, 0)),
                  pl.BlockSpec((1, d), lambda i, k: (0, 0)),
                  pl.BlockSpec((1, d), lambda i, k: (0, 0))],
        out_specs=[pl.BlockSpec((tm, d), lambda i, k: (i, 0)),
                   pl.BlockSpec((tm, d), lambda i, k: (i, 0))],
        out_shape=[jax.ShapeDtypeStruct((m, d), F32), jax.ShapeDtypeStruct((m, d), BF16)],
        scratch_shapes=[pltpu.VMEM((tm, d), F32)],
        compiler_params=_params(("parallel", "arbitrary")),
        name="outproj_residual_ln",
    )(a, w, x, gain.reshape(1, d), bias.reshape(1, d))


def _attention_bias_table(rpb, rows):
    kh = min(NA_MAX_KH, rows)
    cols = jnp.arange(GRID_W)
    col_start = jnp.clip(cols - NA_KW // 2, 0, GRID_W - NA_KW)
    col_in_win = (cols[None, :] >= col_start[:, None]) & (cols[None, :] < col_start[:, None] + NA_KW)
    dc = jnp.clip(cols[None, :] - cols[:, None], -(NA_KW - 1), NA_KW - 1) + NA_KW - 1
    off = jnp.arange(kh)
    dr = jnp.arange(kh)[None, :] - off[:, None] + NA_MAX_KH - 1
    tab = rpb.astype(F32)[:, dr[:, :, None, None], dc[None, None, :, :]]
    tab = jnp.where(col_in_win[None, None, None], tab, -jnp.inf)
    tab = jnp.transpose(tab, (0, 1, 3, 2, 4))
    return tab.reshape(rpb.shape[0], kh, GRID_W, kh * GRID_W)


def _attention_kernel(q_ref, k_ref, v_ref, bias_ref, o_ref, *, rows, kh):
    scale = NA_HEAD_DIM ** -0.5
    for r in range(rows):
        rs = min(max(r - kh // 2, 0), rows - kh)
        q = q_ref[0, r * GRID_W:(r + 1) * GRID_W, :]
        kb = k_ref[0, rs * GRID_W:(rs + kh) * GRID_W, :]
        vb = v_ref[0, rs * GRID_W:(rs + kh) * GRID_W, :]
        s = lax.dot_general(q, kb, (((1,), (1,)), ((), ())), preferred_element_type=F32)
        s = s * scale + bias_ref[0, r - rs]
        m = jnp.max(s, axis=-1, keepdims=True)
        p = jnp.exp(s - m)
        l = jnp.sum(p, axis=-1, keepdims=True)
        o = jnp.dot(p.astype(BF16), vb, preferred_element_type=F32) / l
        o_ref[0, r * GRID_W:(r + 1) * GRID_W, :] = o.astype(o_ref.dtype)


def neighbourhood_attention(qkv, bias_table):
    b, s, d3 = qkv.shape
    d = d3 // 3
    rows = s // GRID_W
    kh = min(NA_MAX_KH, rows)
    nh = NA_HEADS
    hd = NA_HEAD_DIM
    return pl.pallas_call(
        functools.partial(_attention_kernel, rows=rows, kh=kh),
        grid=(nh, b),
        in_specs=[pl.BlockSpec((1, s, hd), lambda h, i: (i, 0, h)),
                  pl.BlockSpec((1, s, hd), lambda h, i: (i, 0, nh + h)),
                  pl.BlockSpec((1, s, hd), lambda h, i: (i, 0, 2 * nh + h)),
                  pl.BlockSpec((1, kh, GRID_W, kh * GRID_W), lambda h, i: (h, 0, 0, 0))],
        out_specs=pl.BlockSpec((1, s, hd), lambda h, i: (i, 0, h)),
        out_shape=jax.ShapeDtypeStruct((b, s, d), BF16),
        compiler_params=_params(("parallel", "arbitrary")),
        name="neighbourhood_attention",
    )(qkv, qkv, qkv, bias_table)


def _retention_kernel(lg_ref, q_ref, k_ref, v_ref, g_ref, cos_ref, sin_ref, o_ref, qr_ref, kr_ref, tb_ref,
                      *, seq, chunk):
    h = pl.program_id(1)
    lg_f = lg_ref[0, h]
    lg_b = lg_ref[1, h]
    half = RET_QK_DIM // 2
    n_chunks = seq // chunk

    cos = cos_ref[...]
    sin = sin_ref[...]
    for src, dst, sc in ((q_ref, qr_ref, 1.0), (k_ref, kr_ref, RET_QK_DIM ** -0.5)):
        t1 = src[0, :, :half].astype(F32)
        t2 = src[0, :, half:].astype(F32)
        dst[:, :half] = ((t1 * cos - t2 * sin) * sc).astype(BF16)
        dst[:, half:] = ((t1 * sin + t2 * cos) * sc).astype(BF16)

    ia = lax.broadcasted_iota(jnp.int32, (chunk, chunk), 0)
    ic = lax.broadcasted_iota(jnp.int32, (chunk, chunk), 1)
    diff = (ia - ic).astype(F32)
    dmat = jnp.where(diff >= 0, jnp.exp(lg_f * jnp.maximum(diff, 0.0)), jnp.exp(lg_b * jnp.maximum(-diff, 0.0)))
    idx = lax.broadcasted_iota(jnp.int32, (chunk, 1), 0).astype(F32)
    qdec_f = jnp.exp(lg_f * (idx + 1.0))
    kdec_f = jnp.exp(lg_f * (chunk - 1.0 - idx))
    qdec_b = jnp.exp(lg_b * (chunk - idx))
    kdec_b = jnp.exp(lg_b * idx)
    cdec_f = jnp.exp(lg_f * chunk)
    cdec_b = jnp.exp(lg_b * chunk)

    def kv_outer(i, kdec):
        ks = (kr_ref[i * chunk:(i + 1) * chunk, :].astype(F32) * kdec).astype(BF16)
        vi = v_ref[0, i * chunk:(i + 1) * chunk, :]
        return lax.dot_general(ks, vi, (((0,), (0,)), ((), ())), preferred_element_type=F32)

    state = jnp.zeros((RET_QK_DIM, RET_V_DIM), F32)
    for i in range(n_chunks - 1, -1, -1):
        tb_ref[i] = state.astype(BF16)
        if i > 0:
            state = state * cdec_b + kv_outer(i, kdec_b)

    state = jnp.zeros((RET_QK_DIM, RET_V_DIM), F32)
    for i in range(n_chunks):
        sl = slice(i * chunk, (i + 1) * chunk)
        qi = qr_ref[sl, :]
        ki = kr_ref[sl, :]
        vi = v_ref[0, sl, :]
        s = lax.dot_general(qi, ki, (((1,), (1,)), ((), ())), preferred_element_type=F32)
        y = jnp.dot((s * dmat).astype(BF16), vi, preferred_element_type=F32)
        y = y + jnp.dot(qi, tb_ref[i], preferred_element_type=F32) * qdec_b
        if i > 0:
            y = y + jnp.dot(qi, state.astype(BF16), preferred_element_type=F32) * qdec_f
        if i < n_chunks - 1:
            state = state * cdec_f + kv_outer(i, kdec_f)
        y = y * lax.rsqrt(jnp.mean(y * y, axis=-1, keepdims=True) + GN_EPS)
        gate = g_ref[0, sl, :].astype(F32)
        gate = gate / (1.0 + jnp.exp(-gate))
        o_ref[0, sl, :] = (gate * y).astype(o_ref.dtype)


def retention(proj, lg, cos, sin):
    b, s, _ = proj.shape
    nh = RET_HEADS
    dk, dv = RET_QK_DIM, RET_V_DIM
    chunk = min(RET_CHUNK, s)
    assert s % chunk == 0
    half = dk // 2
    grid_spec = pltpu.PrefetchScalarGridSpec(
        num_scalar_prefetch=0,
        grid=(b, nh),
        in_specs=[pl.BlockSpec(memory_space=pltpu.SMEM),
                  pl.BlockSpec((1, s, dk), lambda i, h: (i, 0, h)),
                  pl.BlockSpec((1, s, dk), lambda i, h: (i, 0, nh + h)),
                  pl.BlockSpec((1, s, dv), lambda i, h: (i, 0, nh + h)),
                  pl.BlockSpec((1, s, dv), lambda i, h: (i, 0, 2 * nh + h)),
                  pl.BlockSpec((s, half), lambda i, h: (0, 0)),
                  pl.BlockSpec((s, half), lambda i, h: (0, 0))],
        out_specs=pl.BlockSpec((1, s, dv), lambda i, h: (i, 0, h)),
        scratch_shapes=[pltpu.VMEM((s, dk), BF16), pltpu.VMEM((s, dk), BF16),
                        pltpu.VMEM((s // chunk, dk, dv), BF16)],
    )
    return pl.pallas_call(
        functools.partial(_retention_kernel, seq=s, chunk=chunk),
        grid_spec=grid_spec,
        out_shape=jax.ShapeDtypeStruct((b, s, nh * dv), BF16),
        compiler_params=_params(("parallel", "arbitrary")),
        name="retention",
    )(lg, proj, proj, proj, proj, cos, sin)


def _router_kernel(x_ref, wh_ref, wl_ref, b_ref, idx_ref, wts_ref):
    x = x_ref[...]
    xh = x.astype(BF16)
    xl = (x - xh.astype(F32)).astype(BF16)
    logits = (jnp.dot(xh, wh_ref[...], preferred_element_type=F32)
              + jnp.dot(xl, wh_ref[...], preferred_element_type=F32)
              + jnp.dot(xh, wl_ref[...], preferred_element_type=F32)) + b_ref[...]
    lane = lax.broadcasted_iota(jnp.int32, logits.shape, 1)
    neg = -jnp.inf
    big = jnp.int32(ROUTER_LANES)

    gl = jnp.where(lane < N_GROUPS, logits, neg)
    gmax = jnp.max(gl, axis=-1, keepdims=True)
    g_sel = jnp.min(jnp.where(gl == gmax, lane, big), axis=-1, keepdims=True)
    w_grp = 1.0 / jnp.sum(jnp.exp(gl - gmax), axis=-1, keepdims=True)

    lo = N_GROUPS + EXPERTS_PER_GROUP * g_sel
    el = jnp.where((lane >= lo) & (lane < lo + EXPERTS_PER_GROUP), logits, neg)
    v1 = jnp.max(el, axis=-1, keepdims=True)
    i1 = jnp.min(jnp.where(el == v1, lane, big), axis=-1, keepdims=True)
    el2 = jnp.where(lane == i1, neg, el)
    v2 = jnp.max(el2, axis=-1, keepdims=True)
    i2 = jnp.min(jnp.where(el2 == v2, lane, big), axis=-1, keepdims=True)
    e21 = jnp.exp(v2 - v1)
    p1 = 1.0 / (1.0 + e21)
    p2 = e21 * p1
    idx_ref[:, 0:1] = i1 - N_GROUPS
    idx_ref[:, 1:2] = i2 - N_GROUPS
    wts_ref[:, 0:1] = p1 * w_grp
    wts_ref[:, 1:2] = p2 * w_grp


def moe_router(x, w_group, b_group, w_expert, b_expert, *, tm=ROUTER_TM):
    n, d = x.shape
    tm = min(tm, n)
    w = jnp.concatenate([w_group.astype(F32), w_expert.astype(F32).reshape(d, N_EXPERTS)], axis=1)
    w = jnp.pad(w, ((0, 0), (0, ROUTER_LANES - w.shape[1])))
    wh = w.astype(BF16)
    wl = (w - wh.astype(F32)).astype(BF16)
    bias = jnp.concatenate([b_group.astype(F32), b_expert.astype(F32).reshape(N_EXPERTS)])
    bias = jnp.pad(bias, (0, ROUTER_LANES - bias.shape[0])).reshape(1, ROUTER_LANES)
    return pl.pallas_call(
        _router_kernel,
        grid=(n // tm,),
        in_specs=[pl.BlockSpec((tm, d), lambda i: (i, 0)),
                  pl.BlockSpec((d, ROUTER_LANES), lambda i: (0, 0)),
                  pl.BlockSpec((d, ROUTER_LANES), lambda i: (0, 0)),
                  pl.BlockSpec((1, ROUTER_LANES), lambda i: (0, 0))],
        out_specs=[pl.BlockSpec((tm, TOP_K), lambda i: (i, 0)),
                   pl.BlockSpec((tm, TOP_K), lambda i: (i, 0))],
        out_shape=[jax.ShapeDtypeStruct((n, TOP_K), jnp.int32), jax.ShapeDtypeStruct((n, TOP_K), F32)],
        compiler_params=_params(("parallel",)),
        name="moe_router",
    )(x, wh, wl, bias)


def _routing_metadata(expert_idx, tm):
    n = expert_idx.shape[0]
    na = n * TOP_K
    n_tiles = na // tm + N_EXPERTS
    e_flat = expert_idx.reshape(na)
    onehot = (e_flat[:, None] == jnp.arange(N_EXPERTS, dtype=jnp.int32)[None, :]).astype(jnp.int32)
    csum = jnp.cumsum(onehot, axis=0)
    counts = csum[-1]
    rank = jnp.sum(onehot * csum, axis=1) - 1
    padded = ((counts + tm - 1) // tm) * tm
    ends = jnp.cumsum(padded)
    starts = ends - padded
    pos = (starts[e_flat] + rank).astype(jnp.int32)
    total_rows = n_tiles * tm
    token_of = jnp.zeros((total_rows,), jnp.int32).at[pos].set(jnp.arange(na, dtype=jnp.int32) // TOP_K)
    tile_start = jnp.arange(n_tiles, dtype=jnp.int32) * tm
    tile_expert = jnp.searchsorted(ends, tile_start, side="right").astype(jnp.int32)
    tile_valid = (tile_start < ends[-1]).astype(jnp.int32)
    last_expert = jnp.max(jnp.where(counts > 0, jnp.arange(N_EXPERTS, dtype=jnp.int32), 0))
    tile_expert = jnp.where(tile_valid > 0, tile_expert, last_expert)
    return pos, token_of, tile_expert, tile_valid


def _row_gather_start(src_hbm, row_of, base, dst, sem, n_rows):
    def body(r, carry):
        pltpu.make_async_copy(src_hbm.at[pl.ds(row_of(base + r), 1), :], dst.at[pl.ds(r, 1), :], sem).start()
        return carry
    lax.fori_loop(0, n_rows, body, 0, unroll=8)


def _row_gather_wait(src_hbm, dst, sem, n_rows):
    pltpu.make_async_copy(src_hbm.at[pl.ds(0, n_rows), :], dst, sem).wait()


def _grouped_swiglu_kernel(tile_expert_ref, tile_valid_ref, token_of_ref, x_hbm, wg_ref, wu_ref, wd_ref, o_ref,
                           xbuf, sem, *, tm):
    t = pl.program_id(0)

    @pl.when(tile_valid_ref[t] > 0)
    def _():
        _row_gather_start(x_hbm, lambda p: token_of_ref[p], t * tm, xbuf, sem, tm)
        _row_gather_wait(x_hbm, xbuf, sem, tm)
        xb = xbuf[...].astype(BF16)
        hg = jnp.dot(xb, wg_ref[0], preferred_element_type=F32)
        hu = jnp.dot(xb, wu_ref[0], preferred_element_type=F32)
        hcur = (hg / (1.0 + jnp.exp(-hg)) * hu).astype(BF16)
        o_ref[...] = jnp.dot(hcur, wd_ref[0], preferred_element_type=F32)

    @pl.when(tile_valid_ref[t] == 0)
    def _():
        o_ref[...] = jnp.zeros_like(o_ref)


def grouped_swiglu(x, token_of, tile_expert, tile_valid, w_gate, w_up, w_down, *, tm=MOE_TM):
    n, d = x.shape
    f = w_gate.shape[-1]
    n_tiles = tile_expert.shape[0]
    grid_spec = pltpu.PrefetchScalarGridSpec(
        num_scalar_prefetch=3,
        grid=(n_tiles,),
        in_specs=[pl.BlockSpec(memory_space=pl.ANY),
                  pl.BlockSpec((1, d, f), lambda t, te, tv, tok: (te[t], 0, 0)),
                  pl.BlockSpec((1, d, f), lambda t, te, tv, tok: (te[t], 0, 0)),
                  pl.BlockSpec((1, f, d), lambda t, te, tv, tok: (te[t], 0, 0))],
        out_specs=pl.BlockSpec((tm, d), lambda t, te, tv, tok: (t, 0)),
        scratch_shapes=[pltpu.VMEM((tm, d), F32), pltpu.SemaphoreType.DMA],
    )
    return pl.pallas_call(
        functools.partial(_grouped_swiglu_kernel, tm=tm),
        grid_spec=grid_spec,
        out_shape=jax.ShapeDtypeStruct((n_tiles * tm, d), F32),
        compiler_params=_params(("arbitrary",)),
        name="grouped_swiglu",
    )(tile_expert, tile_valid, token_of, x, w_gate, w_up, w_down)


def _combine_ln_kernel(pos_ref, rows_hbm, x_ref, wts_ref, g_ref, b_ref, o32_ref, obf_ref, buf0, buf1, sem,
                       *, tm):
    t = pl.program_id(0)
    _row_gather_start(rows_hbm, lambda a: pos_ref[2 * a], t * tm, buf0, sem.at[0], tm)
    _row_gather_start(rows_hbm, lambda a: pos_ref[2 * a + 1], t * tm, buf1, sem.at[1], tm)
    _row_gather_wait(rows_hbm, buf0, sem.at[0], tm)
    _row_gather_wait(rows_hbm, buf1, sem.at[1], tm)
    w = wts_ref[...]
    z = DEEPNORM_ALPHA * x_ref[...] + w[:, 0:1] * buf0[...] + w[:, 1:2] * buf1[...]
    y = _layer_norm_rows(z, g_ref[...], b_ref[...])
    o32_ref[...] = y
    obf_ref[...] = y.astype(BF16)


def combine_residual_layernorm(rows, pos, x, wts, gain, bias, *, tm=COMBINE_TM):
    n, d = x.shape
    tm = min(tm, n)
    grid_spec = pltpu.PrefetchScalarGridSpec(
        num_scalar_prefetch=1,
        grid=(n // tm,),
        in_specs=[pl.BlockSpec(memory_space=pl.ANY),
                  pl.BlockSpec((tm, d), lambda t, p: (t, 0)),
                  pl.BlockSpec((tm, TOP_K), lambda t, p: (t, 0)),
                  pl.BlockSpec((1, d), lambda t, p: (0, 0)),
                  pl.BlockSpec((1, d), lambda t, p: (0, 0))],
        out_specs=[pl.BlockSpec((tm, d), lambda t, p: (t, 0)),
                   pl.BlockSpec((tm, d), lambda t, p: (t, 0))],
        scratch_shapes=[pltpu.VMEM((tm, d), F32), pltpu.VMEM((tm, d), F32), pltpu.SemaphoreType.DMA((2,))],
    )
    return pl.pallas_call(
        functools.partial(_combine_ln_kernel, tm=tm),
        grid_spec=grid_spec,
        out_shape=[jax.ShapeDtypeStruct((n, d), F32), jax.ShapeDtypeStruct((n, d), BF16)],
        compiler_params=_params(("arbitrary",)),
        name="moe_combine_ln",
    )(pos, rows, x, wts, gain.reshape(1, d), bias.reshape(1, d))


def moe_block(x, w_group, b_group, w_expert, b_expert, w_gate, w_up, w_down, gain, bias, *, tm=MOE_TM):
    d = x.shape[1]
    f = w_gate.shape[-1]
    expert_idx, wts = moe_router(x, w_group, b_group, w_expert, b_expert)
    pos, token_of, tile_expert, tile_valid = _routing_metadata(expert_idx, tm)
    rows = grouped_swiglu(x, token_of, tile_expert, tile_valid,
                          w_gate.reshape(N_EXPERTS, d, f).astype(BF16),
                          w_up.reshape(N_EXPERTS, d, f).astype(BF16),
                          w_down.reshape(N_EXPERTS, f, d).astype(BF16), tm=tm)
    return combine_residual_layernorm(rows, pos, x, wts, gain, bias)


def _rotary_tables(seq):
    half = RET_QK_DIM // 2
    freqs = ROPE_BASE ** (-jnp.arange(half, dtype=F32) / half)
    ang = jnp.arange(seq).astype(F32)[:, None] * freqs[None, :]
    return jnp.cos(ang), jnp.sin(ang)


def kernel(x, attn_w_in, attn_w_out, attn_rpb, ret_w_in, ret_w_out, ret_decay_fwd, ret_decay_bwd,
           moe_w_group_router, moe_b_group_router, moe_w_expert_router, moe_b_expert_router,
           moe_w_gate, moe_w_up, moe_w_down, ln_gain, ln_bias):
    b, s, d = x.shape
    n = b * s
    xf = x.reshape(n, d).astype(F32)
    xb = xf.astype(BF16)
    depth = ln_gain.shape[0]
    for i in range(depth):
        j = i // 2
        if i % 2 == 0:
            qkv = matmul(xb, attn_w_in[j].astype(BF16))
            bias_table = _attention_bias_table(attn_rpb[j], s // GRID_W)
            mixed = neighbourhood_attention(qkv.reshape(b, s, 3 * d), bias_table).reshape(n, d)
            w_out = attn_w_out[j]
        else:
            proj = matmul(xb, ret_w_in[j].astype(BF16))
            lg = jnp.stack([jnp.log1p(-jnp.exp(ret_decay_fwd[j].astype(F32))),
                            jnp.log1p(-jnp.exp(ret_decay_bwd[j].astype(F32)))])
            cos, sin = _rotary_tables(s)
            mixed = retention(proj.reshape(b, s, proj.shape[1]), lg, cos, sin).reshape(n, -1)
            w_out = ret_w_out[j]
        xf, xb = matmul_residual_layernorm(mixed, w_out.astype(BF16), xf, ln_gain[i, 0], ln_bias[i, 0])
        xf, xb = moe_block(xf, moe_w_group_router[i], moe_b_group_router[i], moe_w_expert_router[i],
                           moe_b_expert_router[i], moe_w_gate[i], moe_w_up[i], moe_w_down[i],
                           ln_gain[i, 1], ln_bias[i, 1])
    return xf.reshape(b, s, d).astype(x.dtype)
```

```python
import functools
import math

import jax
import jax.numpy as jnp
from jax import lax
from jax.experimental import pallas as pl
from jax.experimental.pallas import tpu as pltpu

GRID_W = 64
NA_HEADS = 16
NA_HEAD_DIM = 128
NA_MAX_KH = 8
NA_KW = 16
RET_HEADS = 8
RET_QK_DIM = 256
RET_V_DIM = 512
ROPE_BASE = 10000.0
N_GROUPS = 4
EXPERTS_PER_GROUP = 8
N_EXPERTS = N_GROUPS * EXPERTS_PER_GROUP
TOP_K = 2
DEPTH = 2
DEEPNORM_ALPHA = (2 * DEPTH) ** 0.25
LN_EPS = 1e-5
GN_EPS = 1e-6

V7X_LANES = 128
V7X_VMEM_LIMIT_BYTES = 56 * 1024 * 1024

BF16 = jnp.bfloat16
F32 = jnp.float32

MM_TM = 1024
MM_TN = 1024
LN_TM = 512
ROUTER_TM = 512
MOE_TM = 256
COMBINE_TM = 256
RET_CHUNK = 256
ROUTER_LANES = V7X_LANES


def _params(semantics):
    return pltpu.CompilerParams(dimension_semantics=semantics, vmem_limit_bytes=V7X_VMEM_LIMIT_BYTES)


def _matmul_kernel(x_ref, w_ref, o_ref, wb_ref):
    @pl.when(pl.program_id(1) == 0)
    def _():
        wb_ref[...] = w_ref[...].astype(BF16)

    o_ref[...] = jnp.dot(x_ref[...], wb_ref[...], preferred_element_type=F32).astype(o_ref.dtype)


def matmul(x, w, *, tm=MM_TM, tn=MM_TN):
    m, k = x.shape
    n = w.shape[1]
    tm, tn = min(tm, m), min(tn, n)
    assert m % tm == 0 and n % tn == 0
    return pl.pallas_call(
        _matmul_kernel,
        grid=(n // tn, m // tm),
        in_specs=[pl.BlockSpec((tm, k), lambda j, i: (i, 0)),
                  pl.BlockSpec((k, tn), lambda j, i: (0, j))],
        out_specs=pl.BlockSpec((tm, tn), lambda j, i: (i, j)),
        out_shape=jax.ShapeDtypeStruct((m, n), BF16),
        scratch_shapes=[pltpu.VMEM((k, tn), BF16)],
        compiler_params=_params(("parallel", "arbitrary")),
        name="proj_matmul",
    )(x, w)


def _layer_norm_rows(z, gain, bias):
    mu = jnp.mean(z, axis=-1, keepdims=True)
    zc = z - mu
    var = jnp.mean(zc * zc, axis=-1, keepdims=True)
    return zc * lax.rsqrt(var + LN_EPS) * gain + bias


def _mm_res_ln_kernel(a_ref, w_ref, x_ref, g_ref, b_ref, o32_ref, obf_ref, acc_ref, *, nk):
    k = pl.program_id(1)

    @pl.when(k == 0)
    def _():
        acc_ref[...] = jnp.zeros_like(acc_ref)

    acc_ref[...] += jnp.dot(a_ref[...], w_ref[...], preferred_element_type=F32)

    @pl.when(k == nk - 1)
    def _():
        y = _layer_norm_rows(DEEPNORM_ALPHA * x_ref[...] + acc_ref[...], g_ref[...], b_ref[...])
        o32_ref[...] = y
        obf_ref[...] = y.astype(BF16)


def matmul_residual_layernorm(a, w, x, gain, bias, *, tm=LN_TM, tk=2048):
    m, kdim = a.shape
    d = w.shape[1]
    tm = min(tm, m)
    assert m % tm == 0 and kdim % tk == 0
    nk = kdim // tk
    return pl.pallas_call(
        functools.partial(_mm_res_ln_kernel, nk=nk),
        grid=(m // tm, nk),
        in_specs=[pl.BlockSpec((tm, tk), lambda i, k: (i, k)),
                  pl.BlockSpec((tk, d), lambda i, k: (k, 0)),
                  pl.BlockSpec((tm, d), lambda i, k: (i, 0)),
                  pl.BlockSpec((1, d), lambda i, k: (0, 0)),
                  pl.BlockSpec((1, d), lambda i, k: (0, 0))],
        out_specs=[pl.BlockSpec((tm, d), lambda i, k: (i, 0)),
                   pl.BlockSpec((tm, d), lambda i, k: (i, 0))],
        out_shape=[jax.ShapeDtypeStruct((m, d), F32), jax.ShapeDtypeStruct((m, d), BF16)],
        scratch_shapes=[pltpu.VMEM((tm, d), F32)],
        compiler_params=_params(("parallel", "arbitrary")),
        name="outproj_residual_ln",
    )(a, w, x, gain.reshape(1, d), bias.reshape(1, d))


def _attention_bias_table(rpb, rows):
    kh = min(NA_MAX_KH, rows)
    cols = jnp.arange(GRID_W)
    col_start = jnp.clip(cols - NA_KW // 2, 0, GRID_W - NA_KW)
    col_in_win = (cols[None, :] >= col_start[:, None]) & (cols[None, :] < col_start[:, None] + NA_KW)
    dc = jnp.clip(cols[None, :] - cols[:, None], -(NA_KW - 1), NA_KW - 1) + NA_KW - 1
    rpb = rpb.astype(F32)
    nh, n_dr, n_dc = rpb.shape
    cexp = jnp.zeros((nh, GRID_W, n_dr, GRID_W), F32)
    for j in range(n_dc):
        cexp = jnp.where((dc == j)[None, :, None, :], rpb[:, None, :, j, None], cexp)
    cexp = jnp.where(col_in_win[None, :, None, :], cexp, -jnp.inf)
    tab = jnp.stack([cexp[:, :, NA_MAX_KH - 1 - off:NA_MAX_KH - 1 - off + kh, :] for off in range(kh)], axis=1)
    return tab.reshape(nh, kh, GRID_W, kh * GRID_W)


def _attention_kernel(q_ref, k_ref, v_ref, bias_ref, o_ref, *, rows, kh):
    scale = NA_HEAD_DIM ** -0.5
    for r in range(rows):
        rs = min(max(r - kh // 2, 0), rows - kh)
        q = q_ref[0, r * GRID_W:(r + 1) * GRID_W, :]
        kb = k_ref[0, rs * GRID_W:(rs + kh) * GRID_W, :]
        vb = v_ref[0, rs * GRID_W:(rs + kh) * GRID_W, :]
        s = lax.dot_general(q, kb, (((1,), (1,)), ((), ())), preferred_element_type=F32)
        s = s * scale + bias_ref[0, r - rs]
        m = jnp.max(s, axis=-1, keepdims=True)
        p = jnp.exp(s - m)
        l = jnp.sum(p, axis=-1, keepdims=True)
        o = jnp.dot(p.astype(BF16), vb, preferred_element_type=F32) / l
        o_ref[0, r * GRID_W:(r + 1) * GRID_W, :] = o.astype(o_ref.dtype)


def neighbourhood_attention(qkv, bias_table):
    b, s, d3 = qkv.shape
    d = d3 // 3
    rows = s // GRID_W
    kh = min(NA_MAX_KH, rows)
    nh = NA_HEADS
    hd = NA_HEAD_DIM
    return pl.pallas_call(
        functools.partial(_attention_kernel, rows=rows, kh=kh),
        grid=(nh, b),
        in_specs=[pl.BlockSpec((1, s, hd), lambda h, i: (i, 0, h)),
                  pl.BlockSpec((1, s, hd), lambda h, i: (i, 0, nh + h)),
                  pl.BlockSpec((1, s, hd), lambda h, i: (i, 0, 2 * nh + h)),
                  pl.BlockSpec((1, kh, GRID_W, kh * GRID_W), lambda h, i: (h, 0, 0, 0))],
        out_specs=pl.BlockSpec((1, s, hd), lambda h, i: (i, 0, h)),
        out_shape=jax.ShapeDtypeStruct((b, s, d), BF16),
        compiler_params=_params(("parallel", "arbitrary")),
        name="neighbourhood_attention",
    )(qkv, qkv, qkv, bias_table)


def _retention_kernel(lg_ref, q_ref, k_ref, v_ref, g_ref, cos_ref, sin_ref, o_ref, qr_ref, kr_ref, tb_ref,
                      *, seq, chunk):
    h = pl.program_id(1)
    lg_f = lg_ref[0, h]
    lg_b = lg_ref[1, h]
    half = RET_QK_DIM // 2
    n_chunks = seq // chunk

    cos = cos_ref[...]
    sin = sin_ref[...]
    for src, dst, sc in ((q_ref, qr_ref, 1.0), (k_ref, kr_ref, RET_QK_DIM ** -0.5)):
        t1 = src[0, :, :half].astype(F32)
        t2 = src[0, :, half:].astype(F32)
        dst[:, :half] = ((t1 * cos - t2 * sin) * sc).astype(BF16)
        dst[:, half:] = ((t1 * sin + t2 * cos) * sc).astype(BF16)

    ia = lax.broadcasted_iota(jnp.int32, (chunk, chunk), 0)
    ic = lax.broadcasted_iota(jnp.int32, (chunk, chunk), 1)
    diff = (ia - ic).astype(F32)
    dmat = jnp.where(diff >= 0, jnp.exp(lg_f * jnp.maximum(diff, 0.0)), jnp.exp(lg_b * jnp.maximum(-diff, 0.0)))
    idx = lax.broadcasted_iota(jnp.int32, (chunk, 1), 0).astype(F32)
    qdec_f = jnp.exp(lg_f * (idx + 1.0))
    kdec_f = jnp.exp(lg_f * (chunk - 1.0 - idx))
    qdec_b = jnp.exp(lg_b * (chunk - idx))
    kdec_b = jnp.exp(lg_b * idx)
    cdec_f = jnp.exp(lg_f * chunk)
    cdec_b = jnp.exp(lg_b * chunk)

    def kv_outer(i, kdec):
        ks = (kr_ref[i * chunk:(i + 1) * chunk, :].astype(F32) * kdec).astype(BF16)
        vi = v_ref[0, i * chunk:(i + 1) * chunk, :]
        return lax.dot_general(ks, vi, (((0,), (0,)), ((), ())), preferred_element_type=F32)

    state = jnp.zeros((RET_QK_DIM, RET_V_DIM), F32)
    for i in range(n_chunks - 1, -1, -1):
        tb_ref[i] = state.astype(BF16)
        if i > 0:
            state = state * cdec_b + kv_outer(i, kdec_b)

    state = jnp.zeros((RET_QK_DIM, RET_V_DIM), F32)
    for i in range(n_chunks):
        sl = slice(i * chunk, (i + 1) * chunk)
        qi = qr_ref[sl, :]
        ki = kr_ref[sl, :]
        vi = v_ref[0, sl, :]
        s = lax.dot_general(qi, ki, (((1,), (1,)), ((), ())), preferred_element_type=F32)
        y = jnp.dot((s * dmat).astype(BF16), vi, preferred_element_type=F32)
        y = y + jnp.dot(qi, tb_ref[i], preferred_element_type=F32) * qdec_b
        if i > 0:
            y = y + jnp.dot(qi, state.astype(BF16), preferred_element_type=F32) * qdec_f
        if i < n_chunks - 1:
            state = state * cdec_f + kv_outer(i, kdec_f)
        y = y * lax.rsqrt(jnp.mean(y * y, axis=-1, keepdims=True) + GN_EPS)
        gate = g_ref[0, sl, :].astype(F32)
        gate = gate / (1.0 + jnp.exp(-gate))
        o_ref[0, sl, :] = (gate * y).astype(o_ref.dtype)


def retention(proj, lg, cos, sin):
    b, s, _ = proj.shape
    nh = RET_HEADS
    dk, dv = RET_QK_DIM, RET_V_DIM
    chunk = min(RET_CHUNK, s)
    assert s % chunk == 0
    half = dk // 2
    grid_spec = pltpu.PrefetchScalarGridSpec(
        num_scalar_prefetch=0,
        grid=(b, nh),
        in_specs=[pl.BlockSpec(memory_space=pltpu.SMEM),
                  pl.BlockSpec((1, s, dk), lambda i, h: (i, 0, h)),
                  pl.BlockSpec((1, s, dk), lambda i, h: (i, 0, nh + h)),
                  pl.BlockSpec((1, s, dv), lambda i, h: (i, 0, nh + h)),
                  pl.BlockSpec((1, s, dv), lambda i, h: (i, 0, 2 * nh + h)),
                  pl.BlockSpec((s, half), lambda i, h: (0, 0)),
                  pl.BlockSpec((s, half), lambda i, h: (0, 0))],
        out_specs=pl.BlockSpec((1, s, dv), lambda i, h: (i, 0, h)),
        scratch_shapes=[pltpu.VMEM((s, dk), BF16), pltpu.VMEM((s, dk), BF16),
                        pltpu.VMEM((s // chunk, dk, dv), BF16)],
    )
    return pl.pallas_call(
        functools.partial(_retention_kernel, seq=s, chunk=chunk),
        grid_spec=grid_spec,
        out_shape=jax.ShapeDtypeStruct((b, s, nh * dv), BF16),
        compiler_params=_params(("parallel", "arbitrary")),
        name="retention",
    )(lg, proj, proj, proj, proj, cos, sin)


def _router_kernel(x_ref, wh_ref, wl_ref, b_ref, idx_ref, wts_ref):
    x = x_ref[...]
    xh = x.astype(BF16)
    xl = (x - xh.astype(F32)).astype(BF16)
    logits = (jnp.dot(xh, wh_ref[...], preferred_element_type=F32)
              + jnp.dot(xl, wh_ref[...], preferred_element_type=F32)
              + jnp.dot(xh, wl_ref[...], preferred_element_type=F32)) + b_ref[...]
    lane = lax.broadcasted_iota(jnp.int32, logits.shape, 1)
    neg = -jnp.inf
    big = jnp.int32(ROUTER_LANES)

    gl = jnp.where(lane < N_GROUPS, logits, neg)
    gmax = jnp.max(gl, axis=-1, keepdims=True)
    g_sel = jnp.min(jnp.where(gl == gmax, lane, big), axis=-1, keepdims=True)
    w_grp = 1.0 / jnp.sum(jnp.exp(gl - gmax), axis=-1, keepdims=True)

    lo = N_GROUPS + EXPERTS_PER_GROUP * g_sel
    el = jnp.where((lane >= lo) & (lane < lo + EXPERTS_PER_GROUP), logits, neg)
    v1 = jnp.max(el, axis=-1, keepdims=True)
    i1 = jnp.min(jnp.where(el == v1, lane, big), axis=-1, keepdims=True)
    el2 = jnp.where(lane == i1, neg, el)
    v2 = jnp.max(el2, axis=-1, keepdims=True)
    i2 = jnp.min(jnp.where(el2 == v2, lane, big), axis=-1, keepdims=True)
    e21 = jnp.exp(v2 - v1)
    p1 = 1.0 / (1.0 + e21)
    p2 = e21 * p1
    idx_ref[:, 0:1] = i1 - N_GROUPS
    idx_ref[:, 1:2] = i2 - N_GROUPS
    wts_ref[:, 0:1] = p1 * w_grp
    wts_ref[:, 1:2] = p2 * w_grp


def moe_router(x, w_group, b_group, w_expert, b_expert, *, tm=ROUTER_TM):
    n, d = x.shape
    tm = min(tm, n)
    w = jnp.concatenate([w_group.astype(F32), w_expert.astype(F32).reshape(d, N_EXPERTS)], axis=1)
    w = jnp.pad(w, ((0, 0), (0, ROUTER_LANES - w.shape[1])))
    wh = w.astype(BF16)
    wl = (w - wh.astype(F32)).astype(BF16)
    bias = jnp.concatenate([b_group.astype(F32), b_expert.astype(F32).reshape(N_EXPERTS)])
    bias = jnp.pad(bias, (0, ROUTER_LANES - bias.shape[0])).reshape(1, ROUTER_LANES)
    return pl.pallas_call(
        _router_kernel,
        grid=(n // tm,),
        in_specs=[pl.BlockSpec((tm, d), lambda i: (i, 0)),
                  pl.BlockSpec((d, ROUTER_LANES), lambda i: (0, 0)),
                  pl.BlockSpec((d, ROUTER_LANES), lambda i: (0, 0)),
                  pl.BlockSpec((1, ROUTER_LANES), lambda i: (0, 0))],
        out_specs=[pl.BlockSpec((tm, TOP_K), lambda i: (i, 0)),
                   pl.BlockSpec((tm, TOP_K), lambda i: (i, 0))],
        out_shape=[jax.ShapeDtypeStruct((n, TOP_K), jnp.int32), jax.ShapeDtypeStruct((n, TOP_K), F32)],
        compiler_params=_params(("parallel",)),
        name="moe_router",
    )(x, wh, wl, bias)


def _routing_metadata(expert_idx, tm):
    n = expert_idx.shape[0]
    na = n * TOP_K
    n_tiles = na // tm + N_EXPERTS
    e_flat = expert_idx.reshape(na)
    onehot = (e_flat[:, None] == jnp.arange(N_EXPERTS, dtype=jnp.int32)[None, :]).astype(jnp.int32)
    csum = jnp.cumsum(onehot, axis=0)
    counts = csum[-1]
    rank = jnp.sum(onehot * csum, axis=1) - 1
    padded = ((counts + tm - 1) // tm) * tm
    ends = jnp.cumsum(padded)
    starts = ends - padded
    pos = (starts[e_flat] + rank).astype(jnp.int32)
    total_rows = n_tiles * tm
    token_of = jnp.zeros((total_rows,), jnp.int32).at[pos].set(jnp.arange(na, dtype=jnp.int32) // TOP_K)
    tile_start = jnp.arange(n_tiles, dtype=jnp.int32) * tm
    tile_expert = jnp.sum((tile_start[:, None] >= ends[None, :]).astype(jnp.int32), axis=1)
    tile_valid = (tile_start < ends[-1]).astype(jnp.int32)
    last_expert = jnp.max(jnp.where(counts > 0, jnp.arange(N_EXPERTS, dtype=jnp.int32), 0))
    tile_expert = jnp.where(tile_valid > 0, tile_expert, last_expert)
    return pos, token_of, tile_expert, tile_valid


def _row_gather_start(src_hbm, row_of, base, dst, sem, n_rows, *, unrolled=False):
    def start(r):
        pltpu.make_async_copy(src_hbm.at[pl.ds(row_of(base + r), 1), :], dst.at[pl.ds(r, 1), :], sem).start()

    if unrolled:
        for r in range(n_rows):
            start(r)
    else:
        def body(r, carry):
            start(r)
            return carry
        lax.fori_loop(0, n_rows, body, 0, unroll=8)


def _row_gather_wait(src_hbm, dst, sem, n_rows):
    pltpu.make_async_copy(src_hbm.at[pl.ds(0, n_rows), :], dst, sem).wait()


def _grouped_swiglu_kernel(tile_expert_ref, tile_valid_ref, token_of_ref, x_hbm, wg_ref, wu_ref, wd_ref, o_ref,
                           xbuf, wgb, wub, wdb, sem, *, tm):
    t = pl.program_id(0)
    slot = lax.rem(t, 2)
    prev = jnp.maximum(t - 1, 0)
    valid = tile_valid_ref[t] > 0
    tok = lambda p: token_of_ref[p]

    @pl.when(t == 0)
    def _():
        _row_gather_start(x_hbm, tok, 0, xbuf.at[0], sem.at[0], tm)

    @pl.when(valid)
    def _():
        @pl.when((t == 0) | (tile_expert_ref[t] != tile_expert_ref[prev]))
        def _():
            wgb[...] = wg_ref[0].astype(BF16)
            wub[...] = wu_ref[0].astype(BF16)
            wdb[...] = wd_ref[0].astype(BF16)

        _row_gather_wait(x_hbm, xbuf.at[slot], sem.at[slot], tm)
        _row_gather_start(x_hbm, tok, (t + 1) * tm, xbuf.at[1 - slot], sem.at[1 - slot], tm, unrolled=True)
        xb = xbuf[slot].astype(BF16)
        hg = jnp.dot(xb, wgb[...], preferred_element_type=F32)
        hu = jnp.dot(xb, wub[...], preferred_element_type=F32)
        hcur = (hg / (1.0 + jnp.exp(-hg)) * hu).astype(BF16)
        o_ref[...] = jnp.dot(hcur, wdb[...], preferred_element_type=F32)

    @pl.when(jnp.logical_not(valid))
    def _():
        @pl.when((t > 0) & (tile_valid_ref[prev] > 0))
        def _():
            _row_gather_wait(x_hbm, xbuf.at[slot], sem.at[slot], tm)

        o_ref[...] = jnp.zeros_like(o_ref)


def grouped_swiglu(x, token_of, tile_expert, tile_valid, w_gate, w_up, w_down, *, tm=MOE_TM):
    n, d = x.shape
    f = w_gate.shape[-1]
    n_tiles = tile_expert.shape[0]
    grid_spec = pltpu.PrefetchScalarGridSpec(
        num_scalar_prefetch=3,
        grid=(n_tiles,),
        in_specs=[pl.BlockSpec(memory_space=pl.ANY),
                  pl.BlockSpec((1, d, f), lambda t, te, tv, tok: (te[t], 0, 0)),
                  pl.BlockSpec((1, d, f), lambda t, te, tv, tok: (te[t], 0, 0)),
                  pl.BlockSpec((1, f, d), lambda t, te, tv, tok: (te[t], 0, 0))],
        out_specs=pl.BlockSpec((tm, d), lambda t, te, tv, tok: (t, 0)),
        scratch_shapes=[pltpu.VMEM((2, tm, d), F32),
                        pltpu.VMEM((d, f), BF16), pltpu.VMEM((d, f), BF16), pltpu.VMEM((f, d), BF16),
                        pltpu.SemaphoreType.DMA((2,))],
    )
    return pl.pallas_call(
        functools.partial(_grouped_swiglu_kernel, tm=tm),
        grid_spec=grid_spec,
        out_shape=jax.ShapeDtypeStruct((n_tiles * tm, d), F32),
        compiler_params=_params(("arbitrary",)),
        name="grouped_swiglu",
    )(tile_expert, tile_valid, token_of, x, w_gate, w_up, w_down)


def _combine_ln_kernel(pos_ref, rows_hbm, x_ref, wts_ref, g_ref, b_ref, o32_ref, obf_ref, buf0, buf1, sem,
                       *, tm):
    t = pl.program_id(0)
    slot = lax.rem(t, 2)

    def start(tile, sl, unrolled):
        _row_gather_start(rows_hbm, lambda a: pos_ref[2 * a], tile * tm, buf0.at[sl], sem.at[0, sl], tm,
                          unrolled=unrolled)
        _row_gather_start(rows_hbm, lambda a: pos_ref[2 * a + 1], tile * tm, buf1.at[sl], sem.at[1, sl], tm,
                          unrolled=unrolled)

    @pl.when(t == 0)
    def _():
        start(0, 0, False)

    _row_gather_wait(rows_hbm, buf0.at[slot], sem.at[0, slot], tm)
    _row_gather_wait(rows_hbm, buf1.at[slot], sem.at[1, slot], tm)

    @pl.when(t + 1 < pl.num_programs(0))
    def _():
        start(t + 1, 1 - slot, True)

    w = wts_ref[...]
    z = DEEPNORM_ALPHA * x_ref[...] + w[:, 0:1] * buf0[slot] + w[:, 1:2] * buf1[slot]
    y = _layer_norm_rows(z, g_ref[...], b_ref[...])
    o32_ref[...] = y
    obf_ref[...] = y.astype(BF16)


def combine_residual_layernorm(rows, pos, x, wts, gain, bias, *, tm=COMBINE_TM):
    n, d = x.shape
    tm = min(tm, n)
    grid_spec = pltpu.PrefetchScalarGridSpec(
        num_scalar_prefetch=1,
        grid=(n // tm,),
        in_specs=[pl.BlockSpec(memory_space=pl.ANY),
                  pl.BlockSpec((tm, d), lambda t, p: (t, 0)),
                  pl.BlockSpec((tm, TOP_K), lambda t, p: (t, 0)),
                  pl.BlockSpec((1, d), lambda t, p: (0, 0)),
                  pl.BlockSpec((1, d), lambda t, p: (0, 0))],
        out_specs=[pl.BlockSpec((tm, d), lambda t, p: (t, 0)),
                   pl.BlockSpec((tm, d), lambda t, p: (t, 0))],
        scratch_shapes=[pltpu.VMEM((2, tm, d), F32), pltpu.VMEM((2, tm, d), F32),
                        pltpu.SemaphoreType.DMA((2, 2))],
    )
    return pl.pallas_call(
        functools.partial(_combine_ln_kernel, tm=tm),
        grid_spec=grid_spec,
        out_shape=[jax.ShapeDtypeStruct((n, d), F32), jax.ShapeDtypeStruct((n, d), BF16)],
        compiler_params=_params(("arbitrary",)),
        name="moe_combine_ln",
    )(pos, rows, x, wts, gain.reshape(1, d), bias.reshape(1, d))


def moe_block(x, w_group, b_group, w_expert, b_expert, w_gate, w_up, w_down, gain, bias, *, tm=MOE_TM):
    d = x.shape[1]
    f = w_gate.shape[-1]
    expert_idx, wts = moe_router(x, w_group, b_group, w_expert, b_expert)
    pos, token_of, tile_expert, tile_valid = _routing_metadata(expert_idx, tm)
    rows = grouped_swiglu(x, token_of, tile_expert, tile_valid,
                          w_gate.reshape(N_EXPERTS, d, f), w_up.reshape(N_EXPERTS, d, f),
                          w_down.reshape(N_EXPERTS, f, d), tm=tm)
    return combine_residual_layernorm(rows, pos, x, wts, gain, bias)


def _rotary_tables(seq):
    half = RET_QK_DIM // 2
    freqs = ROPE_BASE ** (-jnp.arange(half, dtype=F32) / half)
    ang = jnp.arange(seq).astype(F32)[:, None] * freqs[None, :]
    return jnp.cos(ang), jnp.sin(ang)


def kernel(x, attn_w_in, attn_w_out, attn_rpb, ret_w_in, ret_w_out, ret_decay_fwd, ret_decay_bwd,
           moe_w_group_router, moe_b_group_router, moe_w_expert_router, moe_b_expert_router,
           moe_w_gate, moe_w_up, moe_w_down, ln_gain, ln_bias):
    b, s, d = x.shape
    n = b * s
    xf = x.reshape(n, d).astype(F32)
    xb = xf.astype(BF16)
    depth = ln_gain.shape[0]
    for i in range(depth):
        j = i // 2
        if i % 2 == 0:
            qkv = matmul(xb, attn_w_in[j].astype(F32))
            bias_table = _attention_bias_table(attn_rpb[j], s // GRID_W)
            mixed = neighbourhood_attention(qkv.reshape(b, s, 3 * d), bias_table).reshape(n, d)
            w_out = attn_w_out[j]
        else:
            proj = matmul(xb, ret_w_in[j].astype(F32))
            lg = jnp.stack([jnp.log1p(-jnp.exp(ret_decay_fwd[j].astype(F32))),
                            jnp.log1p(-jnp.exp(ret_decay_bwd[j].astype(F32)))])
            cos, sin = _rotary_tables(s)
            mixed = retention(proj.reshape(b, s, proj.shape[1]), lg, cos, sin).reshape(n, -1)
            w_out = ret_w_out[j]
        xf, xb = matmul_residual_layernorm(mixed, w_out.astype(BF16), xf, ln_gain[i, 0], ln_bias[i, 0])
        xf, xb = moe_block(xf, moe_w_group_router[i], moe_b_group_router[i], moe_w_expert_router[i],
                           moe_b_expert_router[i], moe_w_gate[i], moe_w_up[i], moe_w_down[i],
                           ln_gain[i, 1], ln_bias[i, 1])
    return xf.reshape(b, s, d).astype(x.dtype)
```

```python
import functools
import math

import jax
import jax.numpy as jnp
from jax import lax
from jax.experimental import pallas as pl
from jax.experimental.pallas import tpu as pltpu

GRID_W = 64
NA_HEADS = 16
NA_HEAD_DIM = 128
NA_MAX_KH = 8
NA_KW = 16
RET_HEADS = 8
RET_QK_DIM = 256
RET_V_DIM = 512
ROPE_BASE = 10000.0
N_GROUPS = 4
EXPERTS_PER_GROUP = 8
N_EXPERTS = N_GROUPS * EXPERTS_PER_GROUP
TOP_K = 2
DEPTH = 2
DEEPNORM_ALPHA = (2 * DEPTH) ** 0.25
LN_EPS = 1e-5
GN_EPS = 1e-6

V7X_LANES = 128
V7X_VMEM_LIMIT_BYTES = 56 * 1024 * 1024

BF16 = jnp.bfloat16
F32 = jnp.float32

MM_TM = 1024
MM_TN = 1024
LN_TM = 512
ROUTER_TM = 512
MOE_TM = 256
COMBINE_TM = 256
RET_CHUNK = 256
NA_ROW_GROUP = 4
NA_KEY_ROW_TILE = 4
ROUTER_LANES = V7X_LANES


def _params(semantics):
    return pltpu.CompilerParams(dimension_semantics=semantics, vmem_limit_bytes=V7X_VMEM_LIMIT_BYTES)


def _matmul_kernel(x_ref, w_ref, o_ref, wb_ref):
    @pl.when(pl.program_id(1) == 0)
    def _():
        wb_ref[...] = w_ref[...].astype(BF16)

    o_ref[...] = jnp.dot(x_ref[...], wb_ref[...], preferred_element_type=F32).astype(o_ref.dtype)


def matmul(x, w, *, tm=MM_TM, tn=MM_TN):
    m, k = x.shape
    n = w.shape[1]
    tm, tn = min(tm, m), min(tn, n)
    assert m % tm == 0 and n % tn == 0
    return pl.pallas_call(
        _matmul_kernel,
        grid=(n // tn, m // tm),
        in_specs=[pl.BlockSpec((tm, k), lambda j, i: (i, 0)),
                  pl.BlockSpec((k, tn), lambda j, i: (0, j))],
        out_specs=pl.BlockSpec((tm, tn), lambda j, i: (i, j)),
        out_shape=jax.ShapeDtypeStruct((m, n), BF16),
        scratch_shapes=[pltpu.VMEM((k, tn), BF16)],
        compiler_params=_params(("parallel", "arbitrary")),
        name="proj_matmul",
    )(x, w)


def _layer_norm_rows(z, gain, bias):
    mu = jnp.mean(z, axis=-1, keepdims=True)
    zc = z - mu
    var = jnp.mean(zc * zc, axis=-1, keepdims=True)
    return zc * lax.rsqrt(var + LN_EPS) * gain + bias


def _mm_res_ln_kernel(a_ref, w_ref, x_ref, g_ref, b_ref, o32_ref, obf_ref, acc_ref, *, nk):
    k = pl.program_id(1)

    @pl.when(k == 0)
    def _():
        acc_ref[...] = jnp.zeros_like(acc_ref)

    acc_ref[...] += jnp.dot(a_ref[...], w_ref[...], preferred_element_type=F32)

    @pl.when(k == nk - 1)
    def _():
        y = _layer_norm_rows(DEEPNORM_ALPHA * x_ref[...] + acc_ref[...], g_ref[...], b_ref[...])
        o32_ref[...] = y
        obf_ref[...] = y.astype(BF16)


def matmul_residual_layernorm(a, w, x, gain, bias, *, tm=LN_TM, tk=2048):
    m, kdim = a.shape
    d = w.shape[1]
    tm = min(tm, m)
    assert m % tm == 0 and kdim % tk == 0
    nk = kdim // tk
    return pl.pallas_call(
        functools.partial(_mm_res_ln_kernel, nk=nk),
        grid=(m // tm, nk),
        in_specs=[pl.BlockSpec((tm, tk), lambda i, k: (i, k)),
                  pl.BlockSpec((tk, d), lambda i, k: (k, 0)),
                  pl.BlockSpec((tm, d), lambda i, k: (i, 0)),
                  pl.BlockSpec((1, d), lambda i, k: (0, 0)),
                  pl.BlockSpec((1, d), lambda i, k: (0, 0))],
        out_specs=[pl.BlockSpec((tm, d), lambda i, k: (i, 0)),
                   pl.BlockSpec((tm, d), lambda i, k: (i, 0))],
        out_shape=[jax.ShapeDtypeStruct((m, d), F32), jax.ShapeDtypeStruct((m, d), BF16)],
        scratch_shapes=[pltpu.VMEM((tm, d), F32)],
        compiler_params=_params(("parallel", "arbitrary")),
        name="outproj_residual_ln",
    )(a, w, x, gain.reshape(1, d), bias.reshape(1, d))


def _attention_groups(rows):
    kh = min(NA_MAX_KH, rows)
    g = min(NA_ROW_GROUP, rows)
    assert rows % g == 0
    row_start = lambda r: min(max(r - kh // 2, 0), rows - kh)
    groups, specs = [], []
    for r0 in range(0, rows, g):
        lo = row_start(r0)
        n = row_start(r0 + g - 1) + kh - lo
        n = min(-(-n // NA_KEY_ROW_TILE) * NA_KEY_ROW_TILE, rows)
        lo = min(lo, rows - n)
        spec = (n, tuple((r0 + j - lo, row_start(r0 + j) - lo) for j in range(g)))
        if spec not in specs:
            specs.append(spec)
        groups.append((r0, lo, n, specs.index(spec)))
    return g, groups, specs


def _attention_bias_tables(rpb, rows):
    kh = min(NA_MAX_KH, rows)
    cols = jnp.arange(GRID_W)
    col_start = jnp.clip(cols - NA_KW // 2, 0, GRID_W - NA_KW)
    col_in_win = (cols[None, :] >= col_start[:, None]) & (cols[None, :] < col_start[:, None] + NA_KW)
    dc = jnp.clip(cols[None, :] - cols[:, None], -(NA_KW - 1), NA_KW - 1) + NA_KW - 1
    rpb = rpb.astype(F32)
    nh, n_dr, n_dc = rpb.shape
    cexp = jnp.zeros((nh, n_dr, GRID_W, GRID_W), F32)
    for j in range(n_dc):
        cexp = jnp.where((dc == j)[None, None], rpb[:, :, j, None, None], cexp)
    cexp = jnp.where(col_in_win[None, None], cexp, -jnp.inf)
    masked = jnp.full((nh, GRID_W, GRID_W), -jnp.inf, F32)
    tables = []
    for n, per_row in _attention_groups(rows)[2]:
        blocks = []
        for q_off, win_off in per_row:
            blocks.append(jnp.concatenate(
                [cexp[:, kr - q_off + NA_MAX_KH - 1] if win_off <= kr < win_off + kh else masked
                 for kr in range(n)], axis=-1))
        tables.append(jnp.concatenate(blocks, axis=1))
    return tables


def _attention_kernel(q_ref, k_ref, v_ref, *refs, g, groups):
    bias_refs, o_ref = refs[:-1], refs[-1]
    scale = NA_HEAD_DIM ** -0.5
    for r0, lo, n, tid in groups:
        qs = slice(r0 * GRID_W, (r0 + g) * GRID_W)
        ks = slice(lo * GRID_W, (lo + n) * GRID_W)
        s = lax.dot_general(q_ref[0, qs, :], k_ref[0, ks, :], (((1,), (1,)), ((), ())),
                            preferred_element_type=F32)
        s = s * scale + bias_refs[tid][0]
        m = jnp.max(s, axis=-1, keepdims=True)
        p = jnp.exp(s - m)
        l = jnp.sum(p, axis=-1, keepdims=True)
        o = jnp.dot(p.astype(BF16), v_ref[0, ks, :], preferred_element_type=F32) / l
        o_ref[0, qs, :] = o.astype(o_ref.dtype)


def neighbourhood_attention(qkv, bias_tables):
    b, s, d3 = qkv.shape
    d = d3 // 3
    nh = NA_HEADS
    hd = NA_HEAD_DIM
    g, groups, _ = _attention_groups(s // GRID_W)
    bias_specs = [pl.BlockSpec((1,) + t.shape[1:], lambda h, i: (h, 0, 0)) for t in bias_tables]
    return pl.pallas_call(
        functools.partial(_attention_kernel, g=g, groups=groups),
        grid=(nh, b),
        in_specs=[pl.BlockSpec((1, s, hd), lambda h, i: (i, 0, h)),
                  pl.BlockSpec((1, s, hd), lambda h, i: (i, 0, nh + h)),
                  pl.BlockSpec((1, s, hd), lambda h, i: (i, 0, 2 * nh + h))] + bias_specs,
        out_specs=pl.BlockSpec((1, s, hd), lambda h, i: (i, 0, h)),
        out_shape=jax.ShapeDtypeStruct((b, s, d), BF16),
        compiler_params=_params(("parallel", "arbitrary")),
        name="neighbourhood_attention",
    )(qkv, qkv, qkv, *bias_tables)


def _retention_kernel(lg_ref, q_ref, k_ref, v_ref, g_ref, cos_ref, sin_ref, o_ref, qr_ref, kr_ref, tb_ref,
                      *, seq, chunk):
    h = pl.program_id(1)
    lg_f = lg_ref[0, h]
    lg_b = lg_ref[1, h]
    half = RET_QK_DIM // 2
    n_chunks = seq // chunk

    cos = cos_ref[...]
    sin = sin_ref[...]
    for src, dst, sc in ((q_ref, qr_ref, 1.0), (k_ref, kr_ref, RET_QK_DIM ** -0.5)):
        t1 = src[0, :, :half].astype(F32)
        t2 = src[0, :, half:].astype(F32)
        dst[:, :half] = ((t1 * cos - t2 * sin) * sc).astype(BF16)
        dst[:, half:] = ((t1 * sin + t2 * cos) * sc).astype(BF16)

    ia = lax.broadcasted_iota(jnp.int32, (chunk, chunk), 0)
    ic = lax.broadcasted_iota(jnp.int32, (chunk, chunk), 1)
    diff = (ia - ic).astype(F32)
    dmat = jnp.where(diff >= 0, jnp.exp(lg_f * jnp.maximum(diff, 0.0)), jnp.exp(lg_b * jnp.maximum(-diff, 0.0)))
    idx = lax.broadcasted_iota(jnp.int32, (chunk, 1), 0).astype(F32)
    qdec_f = jnp.exp(lg_f * (idx + 1.0))
    kdec_f = jnp.exp(lg_f * (chunk - 1.0 - idx))
    qdec_b = jnp.exp(lg_b * (chunk - idx))
    kdec_b = jnp.exp(lg_b * idx)
    cdec_f = jnp.exp(lg_f * chunk)
    cdec_b = jnp.exp(lg_b * chunk)

    def kv_outer(i, kdec):
        ks = (kr_ref[i * chunk:(i + 1) * chunk, :].astype(F32) * kdec).astype(BF16)
        vi = v_ref[0, i * chunk:(i + 1) * chunk, :]
        return lax.dot_general(ks, vi, (((0,), (0,)), ((), ())), preferred_element_type=F32)

    state = jnp.zeros((RET_QK_DIM, RET_V_DIM), F32)
    for i in range(n_chunks - 1, -1, -1):
        tb_ref[i] = state.astype(BF16)
        if i > 0:
            state = state * cdec_b + kv_outer(i, kdec_b)

    state = jnp.zeros((RET_QK_DIM, RET_V_DIM), F32)
    for i in range(n_chunks):
        sl = slice(i * chunk, (i + 1) * chunk)
        qi = qr_ref[sl, :]
        ki = kr_ref[sl, :]
        vi = v_ref[0, sl, :]
        s = lax.dot_general(qi, ki, (((1,), (1,)), ((), ())), preferred_element_type=F32)
        y = jnp.dot((s * dmat).astype(BF16), vi, preferred_element_type=F32)
        y = y + jnp.dot(qi, tb_ref[i], preferred_element_type=F32) * qdec_b
        if i > 0:
            y = y + jnp.dot(qi, state.astype(BF16), preferred_element_type=F32) * qdec_f
        if i < n_chunks - 1:
            state = state * cdec_f + kv_outer(i, kdec_f)
        y = y * lax.rsqrt(jnp.mean(y * y, axis=-1, keepdims=True) + GN_EPS)
        gate = g_ref[0, sl, :].astype(F32)
        gate = gate / (1.0 + jnp.exp(-gate))
        o_ref[0, sl, :] = (gate * y).astype(o_ref.dtype)


def retention(proj, lg, cos, sin):
    b, s, _ = proj.shape
    nh = RET_HEADS
    dk, dv = RET_QK_DIM, RET_V_DIM
    chunk = min(RET_CHUNK, s)
    assert s % chunk == 0
    half = dk // 2
    grid_spec = pltpu.PrefetchScalarGridSpec(
        num_scalar_prefetch=0,
        grid=(b, nh),
        in_specs=[pl.BlockSpec(memory_space=pltpu.SMEM),
                  pl.BlockSpec((1, s, dk), lambda i, h: (i, 0, h)),
                  pl.BlockSpec((1, s, dk), lambda i, h: (i, 0, nh + h)),
                  pl.BlockSpec((1, s, dv), lambda i, h: (i, 0, nh + h)),
                  pl.BlockSpec((1, s, dv), lambda i, h: (i, 0, 2 * nh + h)),
                  pl.BlockSpec((s, half), lambda i, h: (0, 0)),
                  pl.BlockSpec((s, half), lambda i, h: (0, 0))],
        out_specs=pl.BlockSpec((1, s, dv), lambda i, h: (i, 0, h)),
        scratch_shapes=[pltpu.VMEM((s, dk), BF16), pltpu.VMEM((s, dk), BF16),
                        pltpu.VMEM((s // chunk, dk, dv), BF16)],
    )
    return pl.pallas_call(
        functools.partial(_retention_kernel, seq=s, chunk=chunk),
        grid_spec=grid_spec,
        out_shape=jax.ShapeDtypeStruct((b, s, nh * dv), BF16),
        compiler_params=_params(("parallel", "arbitrary")),
        name="retention",
    )(lg, proj, proj, proj, proj, cos, sin)


def _router_kernel(x_ref, wh_ref, wl_ref, b_ref, idx_ref, wts_ref):
    x = x_ref[...]
    xh = x.astype(BF16)
    xl = (x - xh.astype(F32)).astype(BF16)
    logits = (jnp.dot(xh, wh_ref[...], preferred_element_type=F32)
              + jnp.dot(xl, wh_ref[...], preferred_element_type=F32)
              + jnp.dot(xh, wl_ref[...], preferred_element_type=F32)) + b_ref[...]
    lane = lax.broadcasted_iota(jnp.int32, logits.shape, 1)
    neg = -jnp.inf
    big = jnp.int32(ROUTER_LANES)

    gl = jnp.where(lane < N_GROUPS, logits, neg)
    gmax = jnp.max(gl, axis=-1, keepdims=True)
    g_sel = jnp.min(jnp.where(gl == gmax, lane, big), axis=-1, keepdims=True)
    w_grp = 1.0 / jnp.sum(jnp.exp(gl - gmax), axis=-1, keepdims=True)

    lo = N_GROUPS + EXPERTS_PER_GROUP * g_sel
    el = jnp.where((lane >= lo) & (lane < lo + EXPERTS_PER_GROUP), logits, neg)
    v1 = jnp.max(el, axis=-1, keepdims=True)
    i1 = jnp.min(jnp.where(el == v1, lane, big), axis=-1, keepdims=True)
    el2 = jnp.where(lane == i1, neg, el)
    v2 = jnp.max(el2, axis=-1, keepdims=True)
    i2 = jnp.min(jnp.where(el2 == v2, lane, big), axis=-1, keepdims=True)
    e21 = jnp.exp(v2 - v1)
    p1 = 1.0 / (1.0 + e21)
    p2 = e21 * p1
    idx_ref[:, 0:1] = i1 - N_GROUPS
    idx_ref[:, 1:2] = i2 - N_GROUPS
    wts_ref[:, 0:1] = p1 * w_grp
    wts_ref[:, 1:2] = p2 * w_grp


def moe_router(x, w_group, b_group, w_expert, b_expert, *, tm=ROUTER_TM):
    n, d = x.shape
    tm = min(tm, n)
    w = jnp.concatenate([w_group.astype(F32), w_expert.astype(F32).reshape(d, N_EXPERTS)], axis=1)
    w = jnp.pad(w, ((0, 0), (0, ROUTER_LANES - w.shape[1])))
    wh = w.astype(BF16)
    wl = (w - wh.astype(F32)).astype(BF16)
    bias = jnp.concatenate([b_group.astype(F32), b_expert.astype(F32).reshape(N_EXPERTS)])
    bias = jnp.pad(bias, (0, ROUTER_LANES - bias.shape[0])).reshape(1, ROUTER_LANES)
    return pl.pallas_call(
        _router_kernel,
        grid=(n // tm,),
        in_specs=[pl.BlockSpec((tm, d), lambda i: (i, 0)),
                  pl.BlockSpec((d, ROUTER_LANES), lambda i: (0, 0)),
                  pl.BlockSpec((d, ROUTER_LANES), lambda i: (0, 0)),
                  pl.BlockSpec((1, ROUTER_LANES), lambda i: (0, 0))],
        out_specs=[pl.BlockSpec((tm, TOP_K), lambda i: (i, 0)),
                   pl.BlockSpec((tm, TOP_K), lambda i: (i, 0))],
        out_shape=[jax.ShapeDtypeStruct((n, TOP_K), jnp.int32), jax.ShapeDtypeStruct((n, TOP_K), F32)],
        compiler_params=_params(("parallel",)),
        name="moe_router",
    )(x, wh, wl, bias)


def _routing_metadata(expert_idx, tm):
    n = expert_idx.shape[0]
    na = n * TOP_K
    n_tiles = na // tm + N_EXPERTS
    e_flat = expert_idx.reshape(na)
    onehot = (e_flat[:, None] == jnp.arange(N_EXPERTS, dtype=jnp.int32)[None, :]).astype(jnp.int32)
    csum = jnp.cumsum(onehot, axis=0)
    counts = csum[-1]
    rank = jnp.sum(onehot * csum, axis=1) - 1
    padded = ((counts + tm - 1) // tm) * tm
    ends = jnp.cumsum(padded)
    starts = ends - padded
    pos = (starts[e_flat] + rank).astype(jnp.int32)
    total_rows = n_tiles * tm
    token_of = jnp.zeros((total_rows,), jnp.int32).at[pos].set(jnp.arange(na, dtype=jnp.int32) // TOP_K)
    tile_start = jnp.arange(n_tiles, dtype=jnp.int32) * tm
    tile_expert = jnp.sum((tile_start[:, None] >= ends[None, :]).astype(jnp.int32), axis=1)
    tile_valid = (tile_start < ends[-1]).astype(jnp.int32)
    last_expert = jnp.max(jnp.where(counts > 0, jnp.arange(N_EXPERTS, dtype=jnp.int32), 0))
    tile_expert = jnp.where(tile_valid > 0, tile_expert, last_expert)
    return pos, token_of, tile_expert, tile_valid


def _row_gather_start(src_hbm, row_of, base, dst, sem, n_rows, *, unrolled=False):
    def start(r):
        pltpu.make_async_copy(src_hbm.at[pl.ds(row_of(base + r), 1), :], dst.at[pl.ds(r, 1), :], sem).start()

    if unrolled:
        for r in range(n_rows):
            start(r)
    else:
        def body(r, carry):
            start(r)
            return carry
        lax.fori_loop(0, n_rows, body, 0, unroll=8)


def _row_gather_wait(src_hbm, dst, sem, n_rows):
    pltpu.make_async_copy(src_hbm.at[pl.ds(0, n_rows), :], dst, sem).wait()


def _grouped_swiglu_kernel(tile_expert_ref, tile_valid_ref, token_of_ref, x_hbm, wg_ref, wu_ref, wd_ref, o_ref,
                           xbuf, wgb, wub, wdb, sem, *, tm):
    t = pl.program_id(0)
    slot = lax.rem(t, 2)
    prev = jnp.maximum(t - 1, 0)
    valid = tile_valid_ref[t] > 0
    tok = lambda p: token_of_ref[p]

    @pl.when(t == 0)
    def _():
        _row_gather_start(x_hbm, tok, 0, xbuf.at[0], sem.at[0], tm)

    @pl.when(valid)
    def _():
        @pl.when((t == 0) | (tile_expert_ref[t] != tile_expert_ref[prev]))
        def _():
            wgb[...] = wg_ref[0].astype(BF16)
            wub[...] = wu_ref[0].astype(BF16)
            wdb[...] = wd_ref[0].astype(BF16)

        _row_gather_wait(x_hbm, xbuf.at[slot], sem.at[slot], tm)
        _row_gather_start(x_hbm, tok, (t + 1) * tm, xbuf.at[1 - slot], sem.at[1 - slot], tm, unrolled=True)
        xb = xbuf[slot].astype(BF16)
        hg = jnp.dot(xb, wgb[...], preferred_element_type=F32)
        hu = jnp.dot(xb, wub[...], preferred_element_type=F32)
        hcur = (hg / (1.0 + jnp.exp(-hg)) * hu).astype(BF16)
        o_ref[...] = jnp.dot(hcur, wdb[...], preferred_element_type=F32)

    @pl.when(jnp.logical_not(valid))
    def _():
        @pl.when((t > 0) & (tile_valid_ref[prev] > 0))
        def _():
            _row_gather_wait(x_hbm, xbuf.at[slot], sem.at[slot], tm)

        o_ref[...] = jnp.zeros_like(o_ref)


def grouped_swiglu(x, token_of, tile_expert, tile_valid, w_gate, w_up, w_down, *, tm=MOE_TM):
    n, d = x.shape
    f = w_gate.shape[-1]
    n_tiles = tile_expert.shape[0]
    grid_spec = pltpu.PrefetchScalarGridSpec(
        num_scalar_prefetch=3,
        grid=(n_tiles,),
        in_specs=[pl.BlockSpec(memory_space=pl.ANY),
                  pl.BlockSpec((1, d, f), lambda t, te, tv, tok: (te[t], 0, 0)),
                  pl.BlockSpec((1, d, f), lambda t, te, tv, tok: (te[t], 0, 0)),
                  pl.BlockSpec((1, f, d), lambda t, te, tv, tok: (te[t], 0, 0))],
        out_specs=pl.BlockSpec((tm, d), lambda t, te, tv, tok: (t, 0)),
        scratch_shapes=[pltpu.VMEM((2, tm, d), F32),
                        pltpu.VMEM((d, f), BF16), pltpu.VMEM((d, f), BF16), pltpu.VMEM((f, d), BF16),
                        pltpu.SemaphoreType.DMA((2,))],
    )
    return pl.pallas_call(
        functools.partial(_grouped_swiglu_kernel, tm=tm),
        grid_spec=grid_spec,
        out_shape=jax.ShapeDtypeStruct((n_tiles * tm, d), F32),
        compiler_params=_params(("arbitrary",)),
        name="grouped_swiglu",
    )(tile_expert, tile_valid, token_of, x, w_gate, w_up, w_down)


def _combine_ln_kernel(pos_ref, rows_hbm, x_ref, wts_ref, g_ref, b_ref, o32_ref, obf_ref, buf0, buf1, sem,
                       *, tm):
    t = pl.program_id(0)
    slot = lax.rem(t, 2)

    def start(tile, sl, unrolled):
        _row_gather_start(rows_hbm, lambda a: pos_ref[2 * a], tile * tm, buf0.at[sl], sem.at[0, sl], tm,
                          unrolled=unrolled)
        _row_gather_start(rows_hbm, lambda a: pos_ref[2 * a + 1], tile * tm, buf1.at[sl], sem.at[1, sl], tm,
                          unrolled=unrolled)

    @pl.when(t == 0)
    def _():
        start(0, 0, False)

    _row_gather_wait(rows_hbm, buf0.at[slot], sem.at[0, slot], tm)
    _row_gather_wait(rows_hbm, buf1.at[slot], sem.at[1, slot], tm)

    @pl.when(t + 1 < pl.num_programs(0))
    def _():
        start(t + 1, 1 - slot, True)

    w = wts_ref[...]
    z = DEEPNORM_ALPHA * x_ref[...] + w[:, 0:1] * buf0[slot] + w[:, 1:2] * buf1[slot]
    y = _layer_norm_rows(z, g_ref[...], b_ref[...])
    o32_ref[...] = y
    obf_ref[...] = y.astype(BF16)


def combine_residual_layernorm(rows, pos, x, wts, gain, bias, *, tm=COMBINE_TM):
    n, d = x.shape
    tm = min(tm, n)
    grid_spec = pltpu.PrefetchScalarGridSpec(
        num_scalar_prefetch=1,
        grid=(n // tm,),
        in_specs=[pl.BlockSpec(memory_space=pl.ANY),
                  pl.BlockSpec((tm, d), lambda t, p: (t, 0)),
                  pl.BlockSpec((tm, TOP_K), lambda t, p: (t, 0)),
                  pl.BlockSpec((1, d), lambda t, p: (0, 0)),
                  pl.BlockSpec((1, d), lambda t, p: (0, 0))],
        out_specs=[pl.BlockSpec((tm, d), lambda t, p: (t, 0)),
                   pl.BlockSpec((tm, d), lambda t, p: (t, 0))],
        scratch_shapes=[pltpu.VMEM((2, tm, d), F32), pltpu.VMEM((2, tm, d), F32),
                        pltpu.SemaphoreType.DMA((2, 2))],
    )
    return pl.pallas_call(
        functools.partial(_combine_ln_kernel, tm=tm),
        grid_spec=grid_spec,
        out_shape=[jax.ShapeDtypeStruct((n, d), F32), jax.ShapeDtypeStruct((n, d), BF16)],
        compiler_params=_params(("arbitrary",)),
        name="moe_combine_ln",
    )(pos, rows, x, wts, gain.reshape(1, d), bias.reshape(1, d))


def moe_block(x, w_group, b_group, w_expert, b_expert, w_gate, w_up, w_down, layer, gain, bias, *, tm=MOE_TM):
    d = x.shape[1]
    f = w_gate.shape[-1]
    expert_idx, wts = moe_router(x, w_group, b_group, w_expert, b_expert)
    pos, token_of, tile_expert, tile_valid = _routing_metadata(expert_idx, tm)
    rows = grouped_swiglu(x, token_of, tile_expert + layer * N_EXPERTS, tile_valid,
                          w_gate.reshape(-1, d, f), w_up.reshape(-1, d, f), w_down.reshape(-1, f, d), tm=tm)
    return combine_residual_layernorm(rows, pos, x, wts, gain, bias)


def _rotary_tables(seq):
    half = RET_QK_DIM // 2
    freqs = ROPE_BASE ** (-jnp.arange(half, dtype=F32) / half)
    ang = jnp.arange(seq).astype(F32)[:, None] * freqs[None, :]
    return jnp.cos(ang), jnp.sin(ang)


def kernel(x, attn_w_in, attn_w_out, attn_rpb, ret_w_in, ret_w_out, ret_decay_fwd, ret_decay_bwd,
           moe_w_group_router, moe_b_group_router, moe_w_expert_router, moe_b_expert_router,
           moe_w_gate, moe_w_up, moe_w_down, ln_gain, ln_bias):
    b, s, d = x.shape
    n = b * s
    xf = x.reshape(n, d).astype(F32)
    xb = xf.astype(BF16)
    depth = ln_gain.shape[0]
    for i in range(depth):
        j = i // 2
        if i % 2 == 0:
            qkv = matmul(xb, attn_w_in[j].astype(F32))
            bias_tables = _attention_bias_tables(attn_rpb[j], s // GRID_W)
            mixed = neighbourhood_attention(qkv.reshape(b, s, 3 * d), bias_tables).reshape(n, d)
            w_out = attn_w_out[j]
        else:
            proj = matmul(xb, ret_w_in[j].astype(F32))
            lg = jnp.stack([jnp.log1p(-jnp.exp(ret_decay_fwd[j].astype(F32))),
                            jnp.log1p(-jnp.exp(ret_decay_bwd[j].astype(F32)))])
            cos, sin = _rotary_tables(s)
            mixed = retention(proj.reshape(b, s, proj.shape[1]), lg, cos, sin).reshape(n, -1)
            w_out = ret_w_out[j]
        xf, xb = matmul_residual_layernorm(mixed, w_out.astype(BF16), xf, ln_gain[i, 0], ln_bias[i, 0])
        xf, xb = moe_block(xf, moe_w_group_router[i], moe_b_group_router[i], moe_w_expert_router[i],
                           moe_b_expert_router[i], moe_w_gate, moe_w_up, moe_w_down, i,
                           ln_gain[i, 1], ln_bias[i, 1])
    return xf.reshape(b, s, d).astype(x.dtype)
```

```python
import functools
import math

import jax
import jax.numpy as jnp
from jax import lax
from jax.experimental import pallas as pl
from jax.experimental.pallas import tpu as pltpu

GRID_W = 64
NA_HEADS = 16
NA_HEAD_DIM = 128
NA_MAX_KH = 8
NA_KW = 16
RET_HEADS = 8
RET_QK_DIM = 256
RET_V_DIM = 512
ROPE_BASE = 10000.0
N_GROUPS = 4
EXPERTS_PER_GROUP = 8
N_EXPERTS = N_GROUPS * EXPERTS_PER_GROUP
TOP_K = 2
DEPTH = 2
DEEPNORM_ALPHA = (2 * DEPTH) ** 0.25
LN_EPS = 1e-5
GN_EPS = 1e-6

V7X_LANES = 128
V7X_VMEM_LIMIT_BYTES = 56 * 1024 * 1024

BF16 = jnp.bfloat16
F32 = jnp.float32

MM_TM = 1024
MM_TN = 1024
LN_TM = 512
ROUTER_TM = 512
MOE_TM = 256
COMBINE_TM = 256
RET_CHUNK = 256
NA_ROW_GROUP = 4
NA_KEY_ROW_TILE = 4
ROUTER_LANES = V7X_LANES


def _params(semantics):
    return pltpu.CompilerParams(dimension_semantics=semantics, vmem_limit_bytes=V7X_VMEM_LIMIT_BYTES)


def _matmul_kernel(x_ref, w_ref, *refs, epilogue, scale_from_tile):
    o_ref, wb_ref = refs[-2:]

    @pl.when(pl.program_id(1) == 0)
    def _():
        wb_ref[...] = w_ref[...].astype(BF16)

    acc = jnp.dot(x_ref[...].astype(BF16), wb_ref[...], preferred_element_type=F32)
    if epilogue == "rotary":
        cos, sin = refs[0][...], refs[1][...]
        scale = jnp.where(pl.program_id(0) >= scale_from_tile, RET_QK_DIM ** -0.5, 1.0).astype(F32)
        half = RET_QK_DIM // 2
        for c0 in range(0, acc.shape[1], RET_QK_DIM):
            t1 = acc[:, c0:c0 + half]
            t2 = acc[:, c0 + half:c0 + RET_QK_DIM]
            o_ref[:, c0:c0 + half] = ((t1 * cos - t2 * sin) * scale).astype(o_ref.dtype)
            o_ref[:, c0 + half:c0 + RET_QK_DIM] = ((t1 * sin + t2 * cos) * scale).astype(o_ref.dtype)
    elif epilogue == "silu":
        o_ref[...] = (acc / (1.0 + jnp.exp(-acc))).astype(o_ref.dtype)
    else:
        o_ref[...] = acc.astype(o_ref.dtype)


def matmul(x, w, *, col_start=0, n_cols=None, epilogue=None, rotary=None, scale_from_col=None,
           tm=MM_TM, tn=MM_TN):
    m, k = x.shape
    n = w.shape[1] - col_start if n_cols is None else n_cols
    tm, tn = min(tm, m), min(tn, n)
    assert m % tm == 0 and n % tn == 0 and col_start % tn == 0
    tile0 = col_start // tn
    in_specs = [pl.BlockSpec((tm, k), lambda j, i: (i, 0)),
                pl.BlockSpec((k, tn), lambda j, i: (0, tile0 + j))]
    operands = [x, w]
    scale_from_tile = None
    if epilogue == "rotary":
        seq = rotary[0].shape[0]
        assert seq % tm == 0 and tn % RET_QK_DIM == 0 and scale_from_col % tn == 0
        in_specs += [pl.BlockSpec((tm, RET_QK_DIM // 2), lambda j, i: (i % (seq // tm), 0))] * 2
        operands += list(rotary)
        scale_from_tile = scale_from_col // tn
    return pl.pallas_call(
        functools.partial(_matmul_kernel, epilogue=epilogue, scale_from_tile=scale_from_tile),
        grid=(n // tn, m // tm),
        in_specs=in_specs,
        out_specs=pl.BlockSpec((tm, tn), lambda j, i: (i, j)),
        out_shape=jax.ShapeDtypeStruct((m, n), BF16),
        scratch_shapes=[pltpu.VMEM((k, tn), BF16)],
        compiler_params=_params(("parallel", "arbitrary")),
        name="proj_matmul",
    )(*operands)


def _layer_norm_rows(z, gain, bias):
    mu = jnp.mean(z, axis=-1, keepdims=True)
    zc = z - mu
    var = jnp.mean(zc * zc, axis=-1, keepdims=True)
    return zc * lax.rsqrt(var + LN_EPS) * gain + bias


LN_ROW_SPLIT = 2


def _mm_res_ln_kernel(a_ref, w_ref, x_ref, g_ref, b_ref, o_ref):
    hm = a_ref.shape[0] // LN_ROW_SPLIT
    for h in range(LN_ROW_SPLIT):
        rows = slice(h * hm, (h + 1) * hm)
        z = DEEPNORM_ALPHA * x_ref[rows, :] + jnp.dot(a_ref[rows, :], w_ref[...], preferred_element_type=F32)
        o_ref[rows, :] = _layer_norm_rows(z, g_ref[...], b_ref[...])


def matmul_residual_layernorm(a, w, x, gain, bias, *, tm=LN_TM):
    m, kdim = a.shape
    d = w.shape[1]
    tm = min(tm, m)
    assert m % tm == 0
    return pl.pallas_call(
        _mm_res_ln_kernel,
        grid=(m // tm,),
        in_specs=[pl.BlockSpec((tm, kdim), lambda i: (i, 0)),
                  pl.BlockSpec((kdim, d), lambda i: (0, 0), pipeline_mode=pl.Buffered(1)),
                  pl.BlockSpec((tm, d), lambda i: (i, 0)),
                  pl.BlockSpec((1, d), lambda i: (0, 0)),
                  pl.BlockSpec((1, d), lambda i: (0, 0))],
        out_specs=pl.BlockSpec((tm, d), lambda i: (i, 0)),
        out_shape=jax.ShapeDtypeStruct((m, d), F32),
        compiler_params=_params(("parallel",)),
        name="outproj_residual_ln",
    )(a, w, x, gain.reshape(1, d), bias.reshape(1, d))


def _attention_groups(rows):
    kh = min(NA_MAX_KH, rows)
    g = min(NA_ROW_GROUP, rows)
    assert rows % g == 0
    row_start = lambda r: min(max(r - kh // 2, 0), rows - kh)
    groups, specs = [], []
    for r0 in range(0, rows, g):
        lo = row_start(r0)
        n = row_start(r0 + g - 1) + kh - lo
        n = min(-(-n // NA_KEY_ROW_TILE) * NA_KEY_ROW_TILE, rows)
        lo = min(lo, rows - n)
        spec = (n, tuple((r0 + j - lo, row_start(r0 + j) - lo) for j in range(g)))
        if spec not in specs:
            specs.append(spec)
        groups.append((r0, lo, n, specs.index(spec)))
    return g, groups, specs


def _attention_bias_tables(rpb, rows):
    kh = min(NA_MAX_KH, rows)
    cols = jnp.arange(GRID_W)
    col_start = jnp.clip(cols - NA_KW // 2, 0, GRID_W - NA_KW)
    col_in_win = (cols[None, :] >= col_start[:, None]) & (cols[None, :] < col_start[:, None] + NA_KW)
    dc = jnp.clip(cols[None, :] - cols[:, None], -(NA_KW - 1), NA_KW - 1) + NA_KW - 1
    rpb = rpb.astype(F32)
    nh, n_dr, n_dc = rpb.shape
    cexp = jnp.zeros((nh, n_dr, GRID_W, GRID_W), F32)
    for j in range(n_dc):
        cexp = jnp.where((dc == j)[None, None], rpb[:, :, j, None, None], cexp)
    cexp = jnp.where(col_in_win[None, None], cexp, -jnp.inf)
    masked = jnp.full((nh, GRID_W, GRID_W), -jnp.inf, F32)
    tables = []
    for n, per_row in _attention_groups(rows)[2]:
        blocks = []
        for q_off, win_off in per_row:
            blocks.append(jnp.concatenate(
                [cexp[:, kr - q_off + NA_MAX_KH - 1] if win_off <= kr < win_off + kh else masked
                 for kr in range(n)], axis=-1))
        tables.append(jnp.concatenate(blocks, axis=1))
    return tables


def _attention_kernel(q_ref, k_ref, v_ref, *refs, g, groups):
    bias_refs, o_ref = refs[:-1], refs[-1]
    scale = NA_HEAD_DIM ** -0.5
    for r0, lo, n, tid in groups:
        qs = slice(r0 * GRID_W, (r0 + g) * GRID_W)
        ks = slice(lo * GRID_W, (lo + n) * GRID_W)
        s = lax.dot_general(q_ref[0, qs, :], k_ref[0, ks, :], (((1,), (1,)), ((), ())),
                            preferred_element_type=F32)
        s = s * scale + bias_refs[tid][0]
        m = jnp.max(s, axis=-1, keepdims=True)
        p = jnp.exp(s - m)
        l = jnp.sum(p, axis=-1, keepdims=True)
        o = jnp.dot(p.astype(BF16), v_ref[0, ks, :], preferred_element_type=F32) / l
        o_ref[0, qs, :] = o.astype(o_ref.dtype)


def neighbourhood_attention(qkv, bias_tables):
    b, s, d3 = qkv.shape
    d = d3 // 3
    nh = NA_HEADS
    hd = NA_HEAD_DIM
    g, groups, _ = _attention_groups(s // GRID_W)
    bias_specs = [pl.BlockSpec((1,) + t.shape[1:], lambda h, i: (h, 0, 0)) for t in bias_tables]
    return pl.pallas_call(
        functools.partial(_attention_kernel, g=g, groups=groups),
        grid=(nh, b),
        in_specs=[pl.BlockSpec((1, s, hd), lambda h, i: (i, 0, h)),
                  pl.BlockSpec((1, s, hd), lambda h, i: (i, 0, nh + h)),
                  pl.BlockSpec((1, s, hd), lambda h, i: (i, 0, 2 * nh + h))] + bias_specs,
        out_specs=pl.BlockSpec((1, s, hd), lambda h, i: (i, 0, h)),
        out_shape=jax.ShapeDtypeStruct((b, s, d), BF16),
        compiler_params=_params(("parallel", "arbitrary")),
        name="neighbourhood_attention",
    )(qkv, qkv, qkv, *bias_tables)


def _retention_kernel(lg_ref, q_ref, k_ref, v_ref, g_ref, o_ref, tb_ref, *, seq, chunk):
    h = pl.program_id(1)
    lg_f = lg_ref[0, h]
    lg_b = lg_ref[1, h]
    n_chunks = seq // chunk

    ia = lax.broadcasted_iota(jnp.int32, (chunk, chunk), 0)
    ic = lax.broadcasted_iota(jnp.int32, (chunk, chunk), 1)
    diff = (ia - ic).astype(F32)
    dmat = jnp.where(diff >= 0, jnp.exp(lg_f * jnp.maximum(diff, 0.0)), jnp.exp(lg_b * jnp.maximum(-diff, 0.0)))
    idx = lax.broadcasted_iota(jnp.int32, (chunk, 1), 0).astype(F32)
    qdec_f = jnp.exp(lg_f * (idx + 1.0))
    kdec_f = jnp.exp(lg_f * (chunk - 1.0 - idx))
    qdec_b = jnp.exp(lg_b * (chunk - idx))
    kdec_b = jnp.exp(lg_b * idx)
    cdec_f = jnp.exp(lg_f * chunk)
    cdec_b = jnp.exp(lg_b * chunk)

    def kv_outer(i, kdec):
        ks = (k_ref[0, i * chunk:(i + 1) * chunk, :].astype(F32) * kdec).astype(BF16)
        vi = v_ref[0, i * chunk:(i + 1) * chunk, :]
        return lax.dot_general(ks, vi, (((0,), (0,)), ((), ())), preferred_element_type=F32)

    state = jnp.zeros((RET_QK_DIM, RET_V_DIM), F32)
    for i in range(n_chunks - 1, -1, -1):
        tb_ref[i] = state.astype(BF16)
        if i > 0:
            state = state * cdec_b + kv_outer(i, kdec_b)

    state = jnp.zeros((RET_QK_DIM, RET_V_DIM), F32)
    for i in range(n_chunks):
        sl = slice(i * chunk, (i + 1) * chunk)
        qi = q_ref[0, sl, :]
        ki = k_ref[0, sl, :]
        vi = v_ref[0, sl, :]
        s = lax.dot_general(qi, ki, (((1,), (1,)), ((), ())), preferred_element_type=F32)
        y = jnp.dot((s * dmat).astype(BF16), vi, preferred_element_type=F32)
        y = y + jnp.dot(qi, tb_ref[i], preferred_element_type=F32) * qdec_b
        if i > 0:
            y = y + jnp.dot(qi, state.astype(BF16), preferred_element_type=F32) * qdec_f
        if i < n_chunks - 1:
            state = state * cdec_f + kv_outer(i, kdec_f)
        y = y * lax.rsqrt(jnp.mean(y * y, axis=-1, keepdims=True) + GN_EPS)
        o_ref[0, sl, :] = (g_ref[0, sl, :].astype(F32) * y).astype(o_ref.dtype)


def retention(qk, v, g, lg):
    b, s, _ = qk.shape
    nh = RET_HEADS
    dk, dv = RET_QK_DIM, RET_V_DIM
    chunk = min(RET_CHUNK, s)
    assert s % chunk == 0
    return pl.pallas_call(
        functools.partial(_retention_kernel, seq=s, chunk=chunk),
        grid=(b, nh),
        in_specs=[pl.BlockSpec(memory_space=pltpu.SMEM),
                  pl.BlockSpec((1, s, dk), lambda i, h: (i, 0, h)),
                  pl.BlockSpec((1, s, dk), lambda i, h: (i, 0, nh + h)),
                  pl.BlockSpec((1, s, dv), lambda i, h: (i, 0, h)),
                  pl.BlockSpec((1, s, dv), lambda i, h: (i, 0, h))],
        out_specs=pl.BlockSpec((1, s, dv), lambda i, h: (i, 0, h)),
        out_shape=jax.ShapeDtypeStruct((b, s, nh * dv), BF16),
        scratch_shapes=[pltpu.VMEM((s // chunk, dk, dv), BF16)],
        compiler_params=_params(("parallel", "arbitrary")),
        name="retention",
    )(lg, qk, qk, v, g)


def _router_kernel(x_ref, whl_ref, b_ref, idx_ref, wts_ref):
    x = x_ref[...]
    xh = x.astype(BF16)
    xl = (x - xh.astype(F32)).astype(BF16)
    both = jnp.dot(xh, whl_ref[...], preferred_element_type=F32)
    logits = (both[:, :ROUTER_LANES] + both[:, ROUTER_LANES:]
              + jnp.dot(xl, whl_ref[:, :ROUTER_LANES], preferred_element_type=F32)) + b_ref[...]
    lane = lax.broadcasted_iota(jnp.int32, logits.shape, 1)
    neg = -jnp.inf
    big = jnp.int32(ROUTER_LANES)

    gl = jnp.where(lane < N_GROUPS, logits, neg)
    gmax = jnp.max(gl, axis=-1, keepdims=True)
    g_sel = jnp.min(jnp.where(gl == gmax, lane, big), axis=-1, keepdims=True)
    w_grp = 1.0 / jnp.sum(jnp.exp(gl - gmax), axis=-1, keepdims=True)

    lo = N_GROUPS + EXPERTS_PER_GROUP * g_sel
    el = jnp.where((lane >= lo) & (lane < lo + EXPERTS_PER_GROUP), logits, neg)
    v1 = jnp.max(el, axis=-1, keepdims=True)
    i1 = jnp.min(jnp.where(el == v1, lane, big), axis=-1, keepdims=True)
    el2 = jnp.where(lane == i1, neg, el)
    v2 = jnp.max(el2, axis=-1, keepdims=True)
    i2 = jnp.min(jnp.where(el2 == v2, lane, big), axis=-1, keepdims=True)
    e21 = jnp.exp(v2 - v1)
    p1 = 1.0 / (1.0 + e21)
    p2 = e21 * p1
    idx_ref[:, 0:1] = i1 - N_GROUPS
    idx_ref[:, 1:2] = i2 - N_GROUPS
    wts_ref[:, 0:1] = p1 * w_grp
    wts_ref[:, 1:2] = p2 * w_grp


def moe_router(x, w_group, b_group, w_expert, b_expert, *, tm=ROUTER_TM):
    n, d = x.shape
    tm = min(tm, n)
    w = jnp.concatenate([w_group.astype(F32), w_expert.astype(F32).reshape(d, N_EXPERTS)], axis=1)
    w = jnp.pad(w, ((0, 0), (0, ROUTER_LANES - w.shape[1])))
    wh = w.astype(BF16)
    wl = (w - wh.astype(F32)).astype(BF16)
    bias = jnp.concatenate([b_group.astype(F32), b_expert.astype(F32).reshape(N_EXPERTS)])
    bias = jnp.pad(bias, (0, ROUTER_LANES - bias.shape[0])).reshape(1, ROUTER_LANES)
    return pl.pallas_call(
        _router_kernel,
        grid=(n // tm,),
        in_specs=[pl.BlockSpec((tm, d), lambda i: (i, 0)),
                  pl.BlockSpec((d, 2 * ROUTER_LANES), lambda i: (0, 0)),
                  pl.BlockSpec((1, ROUTER_LANES), lambda i: (0, 0))],
        out_specs=[pl.BlockSpec((tm, TOP_K), lambda i: (i, 0)),
                   pl.BlockSpec((tm, TOP_K), lambda i: (i, 0))],
        out_shape=[jax.ShapeDtypeStruct((n, TOP_K), jnp.int32), jax.ShapeDtypeStruct((n, TOP_K), F32)],
        compiler_params=_params(("parallel",)),
        name="moe_router",
    )(x, jnp.concatenate([wh, wl], axis=1), bias)


def _invert_positions_kernel(pos_ref, tok_ref, *, n_assign, n_rows):
    assert TOP_K == 2

    def zero(i, carry):
        tok_ref[i] = 0
        return carry
    lax.fori_loop(0, n_rows, zero, 0, unroll=8)

    def scatter(a, carry):
        tok_ref[pos_ref[a]] = lax.shift_right_logical(a, 1)
        return carry
    lax.fori_loop(0, n_assign, scatter, 0, unroll=8)


def invert_positions(pos, n_rows):
    n_assign = pos.shape[0]
    return pl.pallas_call(
        functools.partial(_invert_positions_kernel, n_assign=n_assign, n_rows=n_rows),
        in_specs=[pl.BlockSpec(memory_space=pltpu.SMEM)],
        out_specs=pl.BlockSpec(memory_space=pltpu.SMEM),
        out_shape=jax.ShapeDtypeStruct((n_rows,), jnp.int32),
        name="invert_positions",
    )(pos)


def _routing_metadata(expert_idx, tm):
    n = expert_idx.shape[0]
    na = n * TOP_K
    n_tiles = na // tm + N_EXPERTS
    e_flat = expert_idx.reshape(na)
    onehot = (e_flat[:, None] == jnp.arange(N_EXPERTS, dtype=jnp.int32)[None, :]).astype(jnp.int32)
    csum = jnp.cumsum(onehot, axis=0)
    counts = csum[-1]
    rank = jnp.sum(onehot * csum, axis=1) - 1
    padded = ((counts + tm - 1) // tm) * tm
    ends = jnp.cumsum(padded)
    starts = ends - padded
    pos = (starts[e_flat] + rank).astype(jnp.int32)
    token_of = invert_positions(pos, n_tiles * tm)
    tile_start = jnp.arange(n_tiles, dtype=jnp.int32) * tm
    tile_expert = jnp.sum((tile_start[:, None] >= ends[None, :]).astype(jnp.int32), axis=1)
    tile_valid = (tile_start < ends[-1]).astype(jnp.int32)
    last_expert = jnp.max(jnp.where(counts > 0, jnp.arange(N_EXPERTS, dtype=jnp.int32), 0))
    tile_expert = jnp.where(tile_valid > 0, tile_expert, last_expert)
    return pos, token_of, tile_expert, tile_valid


def _row_gather_start(src_hbm, row_of, base, dst, sem, n_rows, *, unrolled=False):
    def start(r):
        pltpu.make_async_copy(src_hbm.at[pl.ds(row_of(base + r), 1), :], dst.at[pl.ds(r, 1), :], sem).start()

    if unrolled:
        for r in range(n_rows):
            start(r)
    else:
        def body(r, carry):
            start(r)
            return carry
        lax.fori_loop(0, n_rows, body, 0, unroll=8)


def _row_gather_wait(src_hbm, dst, sem, n_rows):
    pltpu.make_async_copy(src_hbm.at[pl.ds(0, n_rows), :], dst, sem).wait()


def _grouped_swiglu_kernel(tile_expert_ref, tile_valid_ref, token_of_ref, x_hbm, wg_ref, wu_ref, wd_ref, o_ref,
                           xbuf, wgb, wub, wdb, sem, *, tm):
    t = pl.program_id(0)
    slot = lax.rem(t, 2)
    prev = jnp.maximum(t - 1, 0)
    valid = tile_valid_ref[t] > 0
    tok = lambda p: token_of_ref[p]

    @pl.when(t == 0)
    def _():
        _row_gather_start(x_hbm, tok, 0, xbuf.at[0], sem.at[0], tm)

    @pl.when(valid)
    def _():
        @pl.when((t == 0) | (tile_expert_ref[t] != tile_expert_ref[prev]))
        def _():
            wgb[...] = wg_ref[0].astype(BF16)
            wub[...] = wu_ref[0].astype(BF16)
            wdb[...] = wd_ref[0].astype(BF16)

        _row_gather_wait(x_hbm, xbuf.at[slot], sem.at[slot], tm)
        _row_gather_start(x_hbm, tok, (t + 1) * tm, xbuf.at[1 - slot], sem.at[1 - slot], tm, unrolled=True)
        xb = xbuf[slot].astype(BF16)
        hg = jnp.dot(xb, wgb[...], preferred_element_type=F32)
        hu = jnp.dot(xb, wub[...], preferred_element_type=F32)
        hcur = (hg / (1.0 + jnp.exp(-hg)) * hu).astype(BF16)
        o_ref[...] = jnp.dot(hcur, wdb[...], preferred_element_type=F32)

    @pl.when(jnp.logical_not(valid))
    def _():
        @pl.when((t > 0) & (tile_valid_ref[prev] > 0))
        def _():
            _row_gather_wait(x_hbm, xbuf.at[slot], sem.at[slot], tm)

        o_ref[...] = jnp.zeros_like(o_ref)


def grouped_swiglu(x, token_of, tile_expert, tile_valid, w_gate, w_up, w_down, *, tm=MOE_TM):
    d, f = w_gate.shape[-2:]
    n_tiles = tile_expert.shape[0]
    grid_spec = pltpu.PrefetchScalarGridSpec(
        num_scalar_prefetch=3,
        grid=(n_tiles,),
        in_specs=[pl.BlockSpec(memory_space=pl.ANY),
                  pl.BlockSpec((1, d, f), lambda t, te, tv, tok: (te[t], 0, 0)),
                  pl.BlockSpec((1, d, f), lambda t, te, tv, tok: (te[t], 0, 0)),
                  pl.BlockSpec((1, f, d), lambda t, te, tv, tok: (te[t], 0, 0))],
        out_specs=pl.BlockSpec((tm, d), lambda t, te, tv, tok: (t, 0)),
        scratch_shapes=[pltpu.VMEM((2, tm, d), F32),
                        pltpu.VMEM((d, f), BF16), pltpu.VMEM((d, f), BF16), pltpu.VMEM((f, d), BF16),
                        pltpu.SemaphoreType.DMA((2,))],
    )
    return pl.pallas_call(
        functools.partial(_grouped_swiglu_kernel, tm=tm),
        grid_spec=grid_spec,
        out_shape=jax.ShapeDtypeStruct((n_tiles * tm, d), F32),
        compiler_params=_params(("arbitrary",)),
        name="grouped_swiglu",
    )(tile_expert, tile_valid, token_of, x, w_gate, w_up, w_down)


def _combine_ln_kernel(pos_ref, rows_hbm, x_ref, wts_ref, g_ref, b_ref, *refs, tm):
    out_refs, (buf0, buf1, sem) = refs[:-3], refs[-3:]
    t = pl.program_id(0)
    slot = lax.rem(t, 2)

    def start(tile, sl, unrolled):
        _row_gather_start(rows_hbm, lambda a: pos_ref[2 * a], tile * tm, buf0.at[sl], sem.at[0, sl], tm,
                          unrolled=unrolled)
        _row_gather_start(rows_hbm, lambda a: pos_ref[2 * a + 1], tile * tm, buf1.at[sl], sem.at[1, sl], tm,
                          unrolled=unrolled)

    @pl.when(t == 0)
    def _():
        start(0, 0, False)

    _row_gather_wait(rows_hbm, buf0.at[slot], sem.at[0, slot], tm)
    _row_gather_wait(rows_hbm, buf1.at[slot], sem.at[1, slot], tm)

    @pl.when(t + 1 < pl.num_programs(0))
    def _():
        start(t + 1, 1 - slot, True)

    w = wts_ref[...]
    z = DEEPNORM_ALPHA * x_ref[...] + w[:, 0:1] * buf0[slot] + w[:, 1:2] * buf1[slot]
    y = _layer_norm_rows(z, g_ref[...], b_ref[...])
    for o_ref in out_refs:
        o_ref[...] = y.astype(o_ref.dtype)


def combine_residual_layernorm(rows, pos, x, wts, gain, bias, *, out_dtypes, tm=COMBINE_TM):
    n, d = x.shape
    tm = min(tm, n)
    grid_spec = pltpu.PrefetchScalarGridSpec(
        num_scalar_prefetch=1,
        grid=(n // tm,),
        in_specs=[pl.BlockSpec(memory_space=pl.ANY),
                  pl.BlockSpec((tm, d), lambda t, p: (t, 0)),
                  pl.BlockSpec((tm, TOP_K), lambda t, p: (t, 0)),
                  pl.BlockSpec((1, d), lambda t, p: (0, 0)),
                  pl.BlockSpec((1, d), lambda t, p: (0, 0))],
        out_specs=[pl.BlockSpec((tm, d), lambda t, p: (t, 0)) for _ in out_dtypes],
        scratch_shapes=[pltpu.VMEM((2, tm, d), F32), pltpu.VMEM((2, tm, d), F32),
                        pltpu.SemaphoreType.DMA((2, 2))],
    )
    return pl.pallas_call(
        functools.partial(_combine_ln_kernel, tm=tm),
        grid_spec=grid_spec,
        out_shape=[jax.ShapeDtypeStruct((n, d), dt) for dt in out_dtypes],
        compiler_params=_params(("arbitrary",)),
        name="moe_combine_ln",
    )(pos, rows, x, wts, gain.reshape(1, d), bias.reshape(1, d))


def moe_block(x, w_group, b_group, w_expert, b_expert, w_gate, w_up, w_down, layer, gain, bias,
              *, out_dtypes, tm=MOE_TM):
    d, f = w_gate.shape[-2:]
    expert_idx, wts = moe_router(x, w_group, b_group, w_expert, b_expert)
    pos, token_of, tile_expert, tile_valid = _routing_metadata(expert_idx, tm)
    rows = grouped_swiglu(x, token_of, tile_expert + layer * N_EXPERTS, tile_valid,
                          w_gate.reshape(-1, d, f), w_up.reshape(-1, d, f), w_down.reshape(-1, f, d), tm=tm)
    return combine_residual_layernorm(rows, pos, x, wts, gain, bias, out_dtypes=out_dtypes)


def _rotary_tables(seq):
    half = RET_QK_DIM // 2
    freqs = ROPE_BASE ** (-jnp.arange(half, dtype=F32) / half)
    ang = jnp.arange(seq).astype(F32)[:, None] * freqs[None, :]
    return jnp.cos(ang), jnp.sin(ang)


def kernel(x, attn_w_in, attn_w_out, attn_rpb, ret_w_in, ret_w_out, ret_decay_fwd, ret_decay_bwd,
           moe_w_group_router, moe_b_group_router, moe_w_expert_router, moe_b_expert_router,
           moe_w_gate, moe_w_up, moe_w_down, ln_gain, ln_bias):
    b, s, d = x.shape
    n = b * s
    xf = x.reshape(n, d).astype(F32)
    xb = xf
    depth = ln_gain.shape[0]
    for i in range(depth):
        j = i // 2
        if i % 2 == 0:
            qkv = matmul(xb, attn_w_in[j].astype(F32))
            bias_tables = _attention_bias_tables(attn_rpb[j], s // GRID_W)
            mixed = neighbourhood_attention(qkv.reshape(b, s, 3 * d), bias_tables).reshape(n, d)
            w_out = attn_w_out[j]
        else:
            w_in = ret_w_in[j].astype(F32)
            qk_w, v_w = 2 * RET_HEADS * RET_QK_DIM, RET_HEADS * RET_V_DIM
            qk = matmul(xb, w_in, col_start=0, n_cols=qk_w, epilogue="rotary", rotary=_rotary_tables(s),
                        scale_from_col=qk_w // 2)
            v = matmul(xb, w_in, col_start=qk_w, n_cols=v_w)
            g = matmul(xb, w_in, col_start=qk_w + v_w, n_cols=v_w, epilogue="silu")
            lg = jnp.stack([jnp.log1p(-jnp.exp(ret_decay_fwd[j].astype(F32))),
                            jnp.log1p(-jnp.exp(ret_decay_bwd[j].astype(F32)))])
            mixed = retention(qk.reshape(b, s, qk_w), v.reshape(b, s, v_w), g.reshape(b, s, v_w),
                              lg).reshape(n, v_w)
            w_out = ret_w_out[j]
        xf = matmul_residual_layernorm(mixed, w_out.astype(BF16), xf, ln_gain[i, 0], ln_bias[i, 0])
        last = i == depth - 1
        outs = moe_block(xf, moe_w_group_router[i], moe_b_group_router[i], moe_w_expert_router[i],
                         moe_b_expert_router[i], moe_w_gate, moe_w_up, moe_w_down, i,
                         ln_gain[i, 1], ln_bias[i, 1], out_dtypes=(F32,) if last else (F32, BF16))
        xf, xb = outs[0], outs[-1]
    return xf.reshape(b, s, d).astype(x.dtype)
```

```python
import functools
import math

import jax
import jax.numpy as jnp
from jax import lax
from jax.experimental import pallas as pl
from jax.experimental.pallas import tpu as pltpu

GRID_W = 64
NA_HEADS = 16
NA_HEAD_DIM = 128
NA_MAX_KH = 8
NA_KW = 16
RET_HEADS = 8
RET_QK_DIM = 256
RET_V_DIM = 512
ROPE_BASE = 10000.0
N_GROUPS = 4
EXPERTS_PER_GROUP = 8
N_EXPERTS = N_GROUPS * EXPERTS_PER_GROUP
TOP_K = 2
DEPTH = 2
DEEPNORM_ALPHA = (2 * DEPTH) ** 0.25
LN_EPS = 1e-5
GN_EPS = 1e-6

V7X_LANES = 128
V7X_VMEM_LIMIT_BYTES = 56 * 1024 * 1024

BF16 = jnp.bfloat16
F32 = jnp.float32

MM_TM = 1024
MM_TN = 1024
LN_TM = 512
MOE_TM = 256
COMBINE_TM = 256
RET_CHUNK = 256
NA_ROW_GROUP = 4
NA_KEY_ROW_TILE = 4
ROUTER_LANES = V7X_LANES


def _params(semantics):
    return pltpu.CompilerParams(dimension_semantics=semantics, vmem_limit_bytes=V7X_VMEM_LIMIT_BYTES)


def _matmul_kernel(x_ref, w_ref, *refs, epilogue, scale_from_tile):
    o_ref, wb_ref = refs[-2:]

    @pl.when(pl.program_id(1) == 0)
    def _():
        wb_ref[...] = w_ref[...].astype(BF16)

    acc = jnp.dot(x_ref[...].astype(BF16), wb_ref[...], preferred_element_type=F32)
    if epilogue == "rotary":
        cos, sin = refs[0][...], refs[1][...]
        scale = jnp.where(pl.program_id(0) >= scale_from_tile, RET_QK_DIM ** -0.5, 1.0).astype(F32)
        half = RET_QK_DIM // 2
        for c0 in range(0, acc.shape[1], RET_QK_DIM):
            t1 = acc[:, c0:c0 + half]
            t2 = acc[:, c0 + half:c0 + RET_QK_DIM]
            o_ref[:, c0:c0 + half] = ((t1 * cos - t2 * sin) * scale).astype(o_ref.dtype)
            o_ref[:, c0 + half:c0 + RET_QK_DIM] = ((t1 * sin + t2 * cos) * scale).astype(o_ref.dtype)
    elif epilogue == "silu":
        o_ref[...] = (acc / (1.0 + jnp.exp(-acc))).astype(o_ref.dtype)
    else:
        o_ref[...] = acc.astype(o_ref.dtype)


def matmul(x, w, *, col_start=0, n_cols=None, epilogue=None, rotary=None, scale_from_col=None,
           tm=MM_TM, tn=MM_TN):
    m, k = x.shape
    n = w.shape[1] - col_start if n_cols is None else n_cols
    tm, tn = min(tm, m), min(tn, n)
    assert m % tm == 0 and n % tn == 0 and col_start % tn == 0
    tile0 = col_start // tn
    in_specs = [pl.BlockSpec((tm, k), lambda j, i: (i, 0)),
                pl.BlockSpec((k, tn), lambda j, i: (0, tile0 + j))]
    operands = [x, w]
    scale_from_tile = None
    if epilogue == "rotary":
        seq = rotary[0].shape[0]
        assert seq % tm == 0 and tn % RET_QK_DIM == 0 and scale_from_col % tn == 0
        in_specs += [pl.BlockSpec((tm, RET_QK_DIM // 2), lambda j, i: (i % (seq // tm), 0))] * 2
        operands += list(rotary)
        scale_from_tile = scale_from_col // tn
    return pl.pallas_call(
        functools.partial(_matmul_kernel, epilogue=epilogue, scale_from_tile=scale_from_tile),
        grid=(n // tn, m // tm),
        in_specs=in_specs,
        out_specs=pl.BlockSpec((tm, tn), lambda j, i: (i, j)),
        out_shape=jax.ShapeDtypeStruct((m, n), BF16),
        scratch_shapes=[pltpu.VMEM((k, tn), BF16)],
        compiler_params=_params(("parallel", "arbitrary")),
        name="proj_matmul",
    )(*operands)


def _layer_norm_rows(z, gain, bias):
    mu = jnp.mean(z, axis=-1, keepdims=True)
    zc = z - mu
    var = jnp.mean(zc * zc, axis=-1, keepdims=True)
    return zc * lax.rsqrt(var + LN_EPS) * gain + bias


LN_ROW_SPLIT = 2


def _mm_res_ln_kernel(a_ref, w_ref, x_ref, g_ref, b_ref, whl_ref, rb_ref, o_ref, idx_ref, wts_ref):
    hm = a_ref.shape[0] // LN_ROW_SPLIT
    for h in range(LN_ROW_SPLIT):
        rows = slice(h * hm, (h + 1) * hm)
        z = DEEPNORM_ALPHA * x_ref[rows, :] + jnp.dot(a_ref[rows, :], w_ref[...], preferred_element_type=F32)
        y = _layer_norm_rows(z, g_ref[...], b_ref[...])
        o_ref[rows, :] = y
        _route_rows(y, whl_ref, rb_ref, idx_ref, wts_ref, rows)


def matmul_residual_layernorm_route(a, w, x, gain, bias, router_whl, router_bias, *, tm=LN_TM):
    m, kdim = a.shape
    d = w.shape[1]
    tm = min(tm, m)
    assert m % tm == 0
    const = lambda i: (0, 0)
    row = lambda i: (i, 0)
    return pl.pallas_call(
        _mm_res_ln_kernel,
        grid=(m // tm,),
        in_specs=[pl.BlockSpec((tm, kdim), row),
                  pl.BlockSpec((kdim, d), const, pipeline_mode=pl.Buffered(1)),
                  pl.BlockSpec((tm, d), row),
                  pl.BlockSpec((1, d), const),
                  pl.BlockSpec((1, d), const),
                  pl.BlockSpec((d, 2 * ROUTER_LANES), const),
                  pl.BlockSpec((1, ROUTER_LANES), const)],
        out_specs=[pl.BlockSpec((tm, d), row), pl.BlockSpec((tm, TOP_K), row), pl.BlockSpec((tm, TOP_K), row)],
        out_shape=[jax.ShapeDtypeStruct((m, d), F32), jax.ShapeDtypeStruct((m, TOP_K), jnp.int32),
                   jax.ShapeDtypeStruct((m, TOP_K), F32)],
        compiler_params=_params(("parallel",)),
        name="outproj_residual_ln_route",
    )(a, w, x, gain.reshape(1, d), bias.reshape(1, d), router_whl, router_bias)


def _attention_groups(rows):
    kh = min(NA_MAX_KH, rows)
    g = min(NA_ROW_GROUP, rows)
    assert rows % g == 0
    row_start = lambda r: min(max(r - kh // 2, 0), rows - kh)
    groups, specs = [], []
    for r0 in range(0, rows, g):
        lo = row_start(r0)
        n = row_start(r0 + g - 1) + kh - lo
        n = min(-(-n // NA_KEY_ROW_TILE) * NA_KEY_ROW_TILE, rows)
        lo = min(lo, rows - n)
        spec = (n, tuple((r0 + j - lo, row_start(r0 + j) - lo) for j in range(g)))
        if spec not in specs:
            specs.append(spec)
        groups.append((r0, lo, n, specs.index(spec)))
    return g, groups, specs


def _attention_bias_tables(rpb, rows):
    kh = min(NA_MAX_KH, rows)
    cols = jnp.arange(GRID_W)
    col_start = jnp.clip(cols - NA_KW // 2, 0, GRID_W - NA_KW)
    col_in_win = (cols[None, :] >= col_start[:, None]) & (cols[None, :] < col_start[:, None] + NA_KW)
    dc = jnp.clip(cols[None, :] - cols[:, None], -(NA_KW - 1), NA_KW - 1) + NA_KW - 1
    rpb = rpb.astype(F32)
    nh, n_dr, n_dc = rpb.shape
    cexp = jnp.zeros((nh, GRID_W, n_dr, GRID_W), F32)
    for j in range(n_dc):
        cexp = jnp.where((dc == j)[None, :, None, :], rpb[:, None, :, j, None], cexp)
    cexp = jnp.where(col_in_win[None, :, None, :], cexp, -jnp.inf).reshape(nh, GRID_W, n_dr * GRID_W)
    tables = []
    for n, per_row in _attention_groups(rows)[2]:
        blocks = []
        for q_off, win_off in per_row:
            first = (win_off - q_off + NA_MAX_KH - 1) * GRID_W
            window = cexp[:, :, first:first + kh * GRID_W]
            blocks.append(jnp.pad(window, ((0, 0), (0, 0), (win_off * GRID_W, (n - win_off - kh) * GRID_W)),
                                  constant_values=-jnp.inf))
        tables.append(jnp.concatenate(blocks, axis=1))
    return tables


def _attention_kernel(q_ref, k_ref, v_ref, *refs, g, groups):
    bias_refs, o_ref = refs[:-1], refs[-1]
    scale = NA_HEAD_DIM ** -0.5
    for r0, lo, n, tid in groups:
        qs = slice(r0 * GRID_W, (r0 + g) * GRID_W)
        ks = slice(lo * GRID_W, (lo + n) * GRID_W)
        s = lax.dot_general(q_ref[0, qs, :], k_ref[0, ks, :], (((1,), (1,)), ((), ())),
                            preferred_element_type=F32)
        s = s * scale + bias_refs[tid][0]
        m = jnp.max(s, axis=-1, keepdims=True)
        p = jnp.exp(s - m)
        l = jnp.sum(p, axis=-1, keepdims=True)
        o = jnp.dot(p.astype(BF16), v_ref[0, ks, :], preferred_element_type=F32) / l
        o_ref[0, qs, :] = o.astype(o_ref.dtype)


def neighbourhood_attention(qkv, bias_tables):
    b, s, d3 = qkv.shape
    d = d3 // 3
    nh = NA_HEADS
    hd = NA_HEAD_DIM
    g, groups, _ = _attention_groups(s // GRID_W)
    bias_specs = [pl.BlockSpec((1,) + t.shape[1:], lambda h, i: (h, 0, 0)) for t in bias_tables]
    return pl.pallas_call(
        functools.partial(_attention_kernel, g=g, groups=groups),
        grid=(nh, b),
        in_specs=[pl.BlockSpec((1, s, hd), lambda h, i: (i, 0, h)),
                  pl.BlockSpec((1, s, hd), lambda h, i: (i, 0, nh + h)),
                  pl.BlockSpec((1, s, hd), lambda h, i: (i, 0, 2 * nh + h))] + bias_specs,
        out_specs=pl.BlockSpec((1, s, hd), lambda h, i: (i, 0, h)),
        out_shape=jax.ShapeDtypeStruct((b, s, d), BF16),
        compiler_params=_params(("parallel", "arbitrary")),
        name="neighbourhood_attention",
    )(qkv, qkv, qkv, *bias_tables)


def _retention_kernel(lg_ref, q_ref, k_ref, v_ref, g_ref, o_ref, tb_ref, *, seq, chunk):
    h = pl.program_id(1)
    lg_f = lg_ref[0, h]
    lg_b = lg_ref[1, h]
    n_chunks = seq // chunk

    ia = lax.broadcasted_iota(jnp.int32, (chunk, chunk), 0)
    ic = lax.broadcasted_iota(jnp.int32, (chunk, chunk), 1)
    diff = (ia - ic).astype(F32)
    dmat = jnp.where(diff >= 0, jnp.exp(lg_f * jnp.maximum(diff, 0.0)), jnp.exp(lg_b * jnp.maximum(-diff, 0.0)))
    idx = lax.broadcasted_iota(jnp.int32, (chunk, 1), 0).astype(F32)
    qdec_f = jnp.exp(lg_f * (idx + 1.0))
    kdec_f = jnp.exp(lg_f * (chunk - 1.0 - idx))
    qdec_b = jnp.exp(lg_b * (chunk - idx))
    kdec_b = jnp.exp(lg_b * idx)
    cdec_f = jnp.exp(lg_f * chunk)
    cdec_b = jnp.exp(lg_b * chunk)

    def kv_outer(i, kdec):
        ks = (k_ref[0, i * chunk:(i + 1) * chunk, :].astype(F32) * kdec).astype(BF16)
        vi = v_ref[0, i * chunk:(i + 1) * chunk, :]
        return lax.dot_general(ks, vi, (((0,), (0,)), ((), ())), preferred_element_type=F32)

    state = jnp.zeros((RET_QK_DIM, RET_V_DIM), F32)
    for i in range(n_chunks - 1, -1, -1):
        tb_ref[i] = state.astype(BF16)
        if i > 0:
            state = state * cdec_b + kv_outer(i, kdec_b)

    state = jnp.zeros((RET_QK_DIM, RET_V_DIM), F32)
    for i in range(n_chunks):
        sl = slice(i * chunk, (i + 1) * chunk)
        qi = q_ref[0, sl, :]
        ki = k_ref[0, sl, :]
        vi = v_ref[0, sl, :]
        s = lax.dot_general(qi, ki, (((1,), (1,)), ((), ())), preferred_element_type=F32)
        y = jnp.dot((s * dmat).astype(BF16), vi, preferred_element_type=F32)
        y = y + jnp.dot(qi, tb_ref[i], preferred_element_type=F32) * qdec_b
        if i > 0:
            y = y + jnp.dot(qi, state.astype(BF16), preferred_element_type=F32) * qdec_f
        if i < n_chunks - 1:
            state = state * cdec_f + kv_outer(i, kdec_f)
        y = y * lax.rsqrt(jnp.mean(y * y, axis=-1, keepdims=True) + GN_EPS)
        o_ref[0, sl, :] = (g_ref[0, sl, :].astype(F32) * y).astype(o_ref.dtype)


def retention(qk, v, g, lg):
    b, s, _ = qk.shape
    nh = RET_HEADS
    dk, dv = RET_QK_DIM, RET_V_DIM
    chunk = min(RET_CHUNK, s)
    assert s % chunk == 0
    return pl.pallas_call(
        functools.partial(_retention_kernel, seq=s, chunk=chunk),
        grid=(b, nh),
        in_specs=[pl.BlockSpec(memory_space=pltpu.SMEM),
                  pl.BlockSpec((1, s, dk), lambda i, h: (i, 0, h)),
                  pl.BlockSpec((1, s, dk), lambda i, h: (i, 0, nh + h)),
                  pl.BlockSpec((1, s, dv), lambda i, h: (i, 0, h)),
                  pl.BlockSpec((1, s, dv), lambda i, h: (i, 0, h))],
        out_specs=pl.BlockSpec((1, s, dv), lambda i, h: (i, 0, h)),
        out_shape=jax.ShapeDtypeStruct((b, s, nh * dv), BF16),
        scratch_shapes=[pltpu.VMEM((s // chunk, dk, dv), BF16)],
        compiler_params=_params(("parallel", "arbitrary")),
        name="retention",
    )(lg, qk, qk, v, g)


def _route_rows(x, whl_ref, b_ref, idx_ref, wts_ref, rows):
    xh = x.astype(BF16)
    xl = (x - xh.astype(F32)).astype(BF16)
    both = jnp.dot(xh, whl_ref[...], preferred_element_type=F32)
    logits = (both[:, :ROUTER_LANES] + both[:, ROUTER_LANES:]
              + jnp.dot(xl, whl_ref[:, :ROUTER_LANES], preferred_element_type=F32)) + b_ref[...]
    lane = lax.broadcasted_iota(jnp.int32, logits.shape, 1)
    neg = -jnp.inf
    big = jnp.int32(ROUTER_LANES)

    gl = jnp.where(lane < N_GROUPS, logits, neg)
    gmax = jnp.max(gl, axis=-1, keepdims=True)
    g_sel = jnp.min(jnp.where(gl == gmax, lane, big), axis=-1, keepdims=True)
    w_grp = 1.0 / jnp.sum(jnp.exp(gl - gmax), axis=-1, keepdims=True)

    lo = N_GROUPS + EXPERTS_PER_GROUP * g_sel
    el = jnp.where((lane >= lo) & (lane < lo + EXPERTS_PER_GROUP), logits, neg)
    v1 = jnp.max(el, axis=-1, keepdims=True)
    i1 = jnp.min(jnp.where(el == v1, lane, big), axis=-1, keepdims=True)
    el2 = jnp.where(lane == i1, neg, el)
    v2 = jnp.max(el2, axis=-1, keepdims=True)
    i2 = jnp.min(jnp.where(el2 == v2, lane, big), axis=-1, keepdims=True)
    e21 = jnp.exp(v2 - v1)
    p1 = 1.0 / (1.0 + e21)
    p2 = e21 * p1
    idx_ref[rows, 0:1] = i1 - N_GROUPS
    idx_ref[rows, 1:2] = i2 - N_GROUPS
    wts_ref[rows, 0:1] = p1 * w_grp
    wts_ref[rows, 1:2] = p2 * w_grp


def _router_operands(w_group, b_group, w_expert, b_expert):
    d = w_group.shape[0]
    w = jnp.concatenate([w_group.astype(F32), w_expert.astype(F32).reshape(d, N_EXPERTS)], axis=1)
    w = jnp.pad(w, ((0, 0), (0, ROUTER_LANES - w.shape[1])))
    wh = w.astype(BF16)
    wl = (w - wh.astype(F32)).astype(BF16)
    bias = jnp.concatenate([b_group.astype(F32), b_expert.astype(F32).reshape(N_EXPERTS)])
    bias = jnp.pad(bias, (0, ROUTER_LANES - bias.shape[0])).reshape(1, ROUTER_LANES)
    return jnp.concatenate([wh, wl], axis=1), bias


def _invert_positions_kernel(pos_ref, tok_ref, *, n_assign, n_rows):
    assert TOP_K == 2

    def zero(i, carry):
        tok_ref[i] = 0
        return carry
    lax.fori_loop(0, n_rows, zero, 0, unroll=8)

    def scatter(a, carry):
        tok_ref[pos_ref[a]] = lax.shift_right_logical(a, 1)
        return carry
    lax.fori_loop(0, n_assign, scatter, 0, unroll=8)


def invert_positions(pos, n_rows):
    n_assign = pos.shape[0]
    return pl.pallas_call(
        functools.partial(_invert_positions_kernel, n_assign=n_assign, n_rows=n_rows),
        in_specs=[pl.BlockSpec(memory_space=pltpu.SMEM)],
        out_specs=pl.BlockSpec(memory_space=pltpu.SMEM),
        out_shape=jax.ShapeDtypeStruct((n_rows,), jnp.int32),
        name="invert_positions",
    )(pos)


def _routing_metadata(expert_idx, tm):
    n = expert_idx.shape[0]
    na = n * TOP_K
    n_tiles = na // tm + N_EXPERTS
    e_flat = expert_idx.reshape(na)
    onehot = (e_flat[:, None] == jnp.arange(N_EXPERTS, dtype=jnp.int32)[None, :]).astype(jnp.int32)
    csum = jnp.cumsum(onehot, axis=0)
    counts = csum[-1]
    rank = jnp.sum(onehot * csum, axis=1) - 1
    padded = ((counts + tm - 1) // tm) * tm
    ends = jnp.cumsum(padded)
    starts = ends - padded
    pos = (starts[e_flat] + rank).astype(jnp.int32)
    token_of = invert_positions(pos, n_tiles * tm)
    tile_start = jnp.arange(n_tiles, dtype=jnp.int32) * tm
    tile_expert = jnp.sum((tile_start[:, None] >= ends[None, :]).astype(jnp.int32), axis=1)
    tile_valid = (tile_start < ends[-1]).astype(jnp.int32)
    last_expert = jnp.max(jnp.where(counts > 0, jnp.arange(N_EXPERTS, dtype=jnp.int32), 0))
    tile_expert = jnp.where(tile_valid > 0, tile_expert, last_expert)
    return pos, token_of, tile_expert, tile_valid


def _row_gather_start(src_hbm, row_of, base, dst, sem, n_rows, *, unrolled=False, first_row=0):
    def start(r):
        pltpu.make_async_copy(src_hbm.at[pl.ds(row_of(base + r), 1), :], dst.at[pl.ds(r, 1), :], sem).start()

    if unrolled:
        for r in range(first_row, first_row + n_rows):
            start(r)
    else:
        def body(r, carry):
            start(r)
            return carry
        lax.fori_loop(first_row, first_row + n_rows, body, 0, unroll=8)


def _row_gather_wait(src_hbm, dst, sem, n_rows):
    pltpu.make_async_copy(src_hbm.at[pl.ds(0, n_rows), :], dst, sem).wait()


GATHER_ISSUE_PIECES = 8


def _grouped_swiglu_kernel(tile_expert_ref, tile_valid_ref, token_of_ref, x_hbm, wg_ref, wu_ref, wd_ref, o_ref,
                           xbuf, wgb, wub, wdb, sem, *, tm):
    t = pl.program_id(0)
    slot = lax.rem(t, 2)
    prev = jnp.maximum(t - 1, 0)
    valid = tile_valid_ref[t] > 0
    tok = lambda p: token_of_ref[p]
    piece = tm // GATHER_ISSUE_PIECES

    def start_next(i):
        _row_gather_start(x_hbm, tok, (t + 1) * tm, xbuf.at[1 - slot], sem.at[1 - slot], piece, unrolled=True,
                          first_row=i * piece)

    @pl.when(t == 0)
    def _():
        _row_gather_start(x_hbm, tok, 0, xbuf.at[0], sem.at[0], tm)

    @pl.when(valid)
    def _():
        @pl.when((t == 0) | (tile_expert_ref[t] != tile_expert_ref[prev]))
        def _():
            wgb[...] = wg_ref[0].astype(BF16)
            wub[...] = wu_ref[0].astype(BF16)
            wdb[...] = wd_ref[0].astype(BF16)

        _row_gather_wait(x_hbm, xbuf.at[slot], sem.at[slot], tm)
        d = o_ref.shape[1]
        xb = xbuf[slot].astype(BF16)
        f = wgb.shape[1]
        halves = []
        for i in range(2):
            cols = slice(i * f // 2, (i + 1) * f // 2)
            start_next(2 * i)
            hg = jnp.dot(xb, wgb[:, cols], preferred_element_type=F32)
            start_next(2 * i + 1)
            hu = jnp.dot(xb, wub[:, cols], preferred_element_type=F32)
            halves.append((hg / (1.0 + jnp.exp(-hg)) * hu).astype(BF16))
        hcur = jnp.concatenate(halves, axis=1)
        for i in range(4):
            cols = slice(i * d // 4, (i + 1) * d // 4)
            start_next(4 + i)
            o_ref[:, cols] = jnp.dot(hcur, wdb[:, cols], preferred_element_type=F32)

    @pl.when(jnp.logical_not(valid))
    def _():
        @pl.when((t > 0) & (tile_valid_ref[prev] > 0))
        def _():
            _row_gather_wait(x_hbm, xbuf.at[slot], sem.at[slot], tm)

        o_ref[...] = jnp.zeros_like(o_ref)


def grouped_swiglu(x, token_of, tile_expert, tile_valid, w_gate, w_up, w_down, *, tm=MOE_TM):
    d, f = w_gate.shape[-2:]
    n_tiles = tile_expert.shape[0]
    grid_spec = pltpu.PrefetchScalarGridSpec(
        num_scalar_prefetch=3,
        grid=(n_tiles,),
        in_specs=[pl.BlockSpec(memory_space=pl.ANY),
                  pl.BlockSpec((1, d, f), lambda t, te, tv, tok: (te[t], 0, 0)),
                  pl.BlockSpec((1, d, f), lambda t, te, tv, tok: (te[t], 0, 0)),
                  pl.BlockSpec((1, f, d), lambda t, te, tv, tok: (te[t], 0, 0))],
        out_specs=pl.BlockSpec((tm, d), lambda t, te, tv, tok: (t, 0)),
        scratch_shapes=[pltpu.VMEM((2, tm, d), F32),
                        pltpu.VMEM((d, f), BF16), pltpu.VMEM((d, f), BF16), pltpu.VMEM((f, d), BF16),
                        pltpu.SemaphoreType.DMA((2,))],
    )
    return pl.pallas_call(
        functools.partial(_grouped_swiglu_kernel, tm=tm),
        grid_spec=grid_spec,
        out_shape=jax.ShapeDtypeStruct((n_tiles * tm, d), F32),
        compiler_params=_params(("arbitrary",)),
        name="grouped_swiglu",
    )(tile_expert, tile_valid, token_of, x, w_gate, w_up, w_down)


def _combine_ln_kernel(pos_ref, rows_hbm, x_ref, wts_ref, g_ref, b_ref, *refs, tm):
    out_refs, (buf0, buf1, sem) = refs[:-3], refs[-3:]
    t = pl.program_id(0)
    slot = lax.rem(t, 2)

    def start(tile, sl, unrolled):
        _row_gather_start(rows_hbm, lambda a: pos_ref[2 * a], tile * tm, buf0.at[sl], sem.at[0, sl], tm,
                          unrolled=unrolled)
        _row_gather_start(rows_hbm, lambda a: pos_ref[2 * a + 1], tile * tm, buf1.at[sl], sem.at[1, sl], tm,
                          unrolled=unrolled)

    @pl.when(t == 0)
    def _():
        start(0, 0, False)

    _row_gather_wait(rows_hbm, buf0.at[slot], sem.at[0, slot], tm)
    _row_gather_wait(rows_hbm, buf1.at[slot], sem.at[1, slot], tm)

    @pl.when(t + 1 < pl.num_programs(0))
    def _():
        start(t + 1, 1 - slot, True)

    w = wts_ref[...]
    z = DEEPNORM_ALPHA * x_ref[...] + w[:, 0:1] * buf0[slot] + w[:, 1:2] * buf1[slot]
    y = _layer_norm_rows(z, g_ref[...], b_ref[...])
    for o_ref in out_refs:
        o_ref[...] = y.astype(o_ref.dtype)


def combine_residual_layernorm(rows, pos, x, wts, gain, bias, *, out_dtypes, tm=COMBINE_TM):
    n, d = x.shape
    tm = min(tm, n)
    grid_spec = pltpu.PrefetchScalarGridSpec(
        num_scalar_prefetch=1,
        grid=(n // tm,),
        in_specs=[pl.BlockSpec(memory_space=pl.ANY),
                  pl.BlockSpec((tm, d), lambda t, p: (t, 0)),
                  pl.BlockSpec((tm, TOP_K), lambda t, p: (t, 0)),
                  pl.BlockSpec((1, d), lambda t, p: (0, 0)),
                  pl.BlockSpec((1, d), lambda t, p: (0, 0))],
        out_specs=[pl.BlockSpec((tm, d), lambda t, p: (t, 0)) for _ in out_dtypes],
        scratch_shapes=[pltpu.VMEM((2, tm, d), F32), pltpu.VMEM((2, tm, d), F32),
                        pltpu.SemaphoreType.DMA((2, 2))],
    )
    return pl.pallas_call(
        functools.partial(_combine_ln_kernel, tm=tm),
        grid_spec=grid_spec,
        out_shape=[jax.ShapeDtypeStruct((n, d), dt) for dt in out_dtypes],
        compiler_params=_params(("arbitrary",)),
        name="moe_combine_ln",
    )(pos, rows, x, wts, gain.reshape(1, d), bias.reshape(1, d))


def moe_block(x, expert_idx, wts, w_gate, w_up, w_down, layer, gain, bias, *, out_dtypes, tm=MOE_TM):
    d, f = w_gate.shape[-2:]
    pos, token_of, tile_expert, tile_valid = _routing_metadata(expert_idx, tm)
    rows = grouped_swiglu(x, token_of, tile_expert + layer * N_EXPERTS, tile_valid,
                          w_gate.reshape(-1, d, f), w_up.reshape(-1, d, f), w_down.reshape(-1, f, d), tm=tm)
    return combine_residual_layernorm(rows, pos, x, wts, gain, bias, out_dtypes=out_dtypes)


def _rotary_tables(seq):
    half = RET_QK_DIM // 2
    freqs = ROPE_BASE ** (-jnp.arange(half, dtype=F32) / half)
    ang = jnp.arange(seq).astype(F32)[:, None] * freqs[None, :]
    return jnp.cos(ang), jnp.sin(ang)


def kernel(x, attn_w_in, attn_w_out, attn_rpb, ret_w_in, ret_w_out, ret_decay_fwd, ret_decay_bwd,
           moe_w_group_router, moe_b_group_router, moe_w_expert_router, moe_b_expert_router,
           moe_w_gate, moe_w_up, moe_w_down, ln_gain, ln_bias):
    b, s, d = x.shape
    n = b * s
    xf = x.reshape(n, d).astype(F32)
    xb = xf
    depth = ln_gain.shape[0]
    for i in range(depth):
        j = i // 2
        if i % 2 == 0:
            qkv = matmul(xb, attn_w_in[j].astype(F32))
            bias_tables = _attention_bias_tables(attn_rpb[j], s // GRID_W)
            mixed = neighbourhood_attention(qkv.reshape(b, s, 3 * d), bias_tables).reshape(n, d)
            w_out = attn_w_out[j]
        else:
            w_in = ret_w_in[j].astype(F32)
            qk_w, v_w = 2 * RET_HEADS * RET_QK_DIM, RET_HEADS * RET_V_DIM
            qk = matmul(xb, w_in, col_start=0, n_cols=qk_w, epilogue="rotary", rotary=_rotary_tables(s),
                        scale_from_col=qk_w // 2)
            v = matmul(xb, w_in, col_start=qk_w, n_cols=v_w)
            g = matmul(xb, w_in, col_start=qk_w + v_w, n_cols=v_w, epilogue="silu")
            lg = jnp.stack([jnp.log1p(-jnp.exp(ret_decay_fwd[j].astype(F32))),
                            jnp.log1p(-jnp.exp(ret_decay_bwd[j].astype(F32)))])
            mixed = retention(qk.reshape(b, s, qk_w), v.reshape(b, s, v_w), g.reshape(b, s, v_w),
                              lg).reshape(n, v_w)
            w_out = ret_w_out[j]
        whl, rbias = _router_operands(moe_w_group_router[i], moe_b_group_router[i], moe_w_expert_router[i],
                                      moe_b_expert_router[i])
        xf, expert_idx, wts = matmul_residual_layernorm_route(mixed, w_out.astype(BF16), xf, ln_gain[i, 0],
                                                              ln_bias[i, 0], whl, rbias)
        last = i == depth - 1
        outs = moe_block(xf, expert_idx, wts, moe_w_gate, moe_w_up, moe_w_down, i,
                         ln_gain[i, 1], ln_bias[i, 1], out_dtypes=(F32,) if last else (F32, BF16))
        xf, xb = outs[0], outs[-1]
    return xf.reshape(b, s, d).astype(x.dtype)
```

```python
import functools
import math

import jax
import jax.numpy as jnp
from jax import lax
from jax.experimental import pallas as pl
from jax.experimental.pallas import tpu as pltpu

GRID_W = 64
NA_HEADS = 16
NA_HEAD_DIM = 128
NA_MAX_KH = 8
NA_KW = 16
RET_HEADS = 8
RET_QK_DIM = 256
RET_V_DIM = 512
ROPE_BASE = 10000.0
N_GROUPS = 4
EXPERTS_PER_GROUP = 8
N_EXPERTS = N_GROUPS * EXPERTS_PER_GROUP
TOP_K = 2
DEPTH = 2
DEEPNORM_ALPHA = (2 * DEPTH) ** 0.25
LN_EPS = 1e-5
GN_EPS = 1e-6

V7X_LANES = 128
V7X_VMEM_LIMIT_BYTES = 56 * 1024 * 1024

BF16 = jnp.bfloat16
F32 = jnp.float32

MM_TM = 1024
MM_TN = 1024
LN_TM = 512
MOE_TM = 256
COMBINE_TM = 256
RET_CHUNK = 256
NA_ROW_GROUP = 4
NA_KEY_ROW_TILE = 4
ROUTER_LANES = V7X_LANES


def _params(semantics):
    return pltpu.CompilerParams(dimension_semantics=semantics, vmem_limit_bytes=V7X_VMEM_LIMIT_BYTES)


def _matmul_kernel(x_ref, w_ref, *refs, epilogue, scale_from_tile):
    o_ref, wb_ref = refs[-2:]

    @pl.when(pl.program_id(1) == 0)
    def _():
        wb_ref[...] = w_ref[...].astype(BF16)

    acc = jnp.dot(x_ref[...].astype(BF16), wb_ref[...], preferred_element_type=F32)
    if epilogue == "rotary":
        cos, sin = refs[0][...], refs[1][...]
        scale = jnp.where(pl.program_id(0) >= scale_from_tile, RET_QK_DIM ** -0.5, 1.0).astype(F32)
        half = RET_QK_DIM // 2
        for c0 in range(0, acc.shape[1], RET_QK_DIM):
            t1 = acc[:, c0:c0 + half]
            t2 = acc[:, c0 + half:c0 + RET_QK_DIM]
            o_ref[:, c0:c0 + half] = ((t1 * cos - t2 * sin) * scale).astype(o_ref.dtype)
            o_ref[:, c0 + half:c0 + RET_QK_DIM] = ((t1 * sin + t2 * cos) * scale).astype(o_ref.dtype)
    elif epilogue == "silu":
        o_ref[...] = (acc / (1.0 + jnp.exp(-acc))).astype(o_ref.dtype)
    else:
        o_ref[...] = acc.astype(o_ref.dtype)


def matmul(x, w, *, col_start=0, n_cols=None, epilogue=None, rotary=None, scale_from_col=None,
           tm=MM_TM, tn=MM_TN):
    m, k = x.shape
    n = w.shape[1] - col_start if n_cols is None else n_cols
    tm, tn = min(tm, m), min(tn, n)
    assert m % tm == 0 and n % tn == 0 and col_start % tn == 0
    tile0 = col_start // tn
    in_specs = [pl.BlockSpec((tm, k), lambda j, i: (i, 0)),
                pl.BlockSpec((k, tn), lambda j, i: (0, tile0 + j))]
    operands = [x, w]
    scale_from_tile = None
    if epilogue == "rotary":
        seq = rotary[0].shape[0]
        assert seq % tm == 0 and tn % RET_QK_DIM == 0 and scale_from_col % tn == 0
        in_specs += [pl.BlockSpec((tm, RET_QK_DIM // 2), lambda j, i: (i % (seq // tm), 0))] * 2
        operands += list(rotary)
        scale_from_tile = scale_from_col // tn
    return pl.pallas_call(
        functools.partial(_matmul_kernel, epilogue=epilogue, scale_from_tile=scale_from_tile),
        grid=(n // tn, m // tm),
        in_specs=in_specs,
        out_specs=pl.BlockSpec((tm, tn), lambda j, i: (i, j)),
        out_shape=jax.ShapeDtypeStruct((m, n), BF16),
        scratch_shapes=[pltpu.VMEM((k, tn), BF16)],
        compiler_params=_params(("parallel", "arbitrary")),
        name="proj_matmul",
    )(*operands)


def _layer_norm_rows(z, gain, bias):
    mu = jnp.mean(z, axis=-1, keepdims=True)
    zc = z - mu
    var = jnp.mean(zc * zc, axis=-1, keepdims=True)
    return zc * lax.rsqrt(var + LN_EPS) * gain + bias


LN_ROW_SPLIT = 2


def _mm_res_ln_kernel(a_ref, w_ref, x_ref, g_ref, b_ref, whl_ref, rb_ref, o_ref, idx_ref, wts_ref):
    hm = a_ref.shape[0] // LN_ROW_SPLIT
    for h in range(LN_ROW_SPLIT):
        rows = slice(h * hm, (h + 1) * hm)
        z = DEEPNORM_ALPHA * x_ref[rows, :] + jnp.dot(a_ref[rows, :], w_ref[...], preferred_element_type=F32)
        y = _layer_norm_rows(z, g_ref[...], b_ref[...])
        o_ref[rows, :] = y
        _route_rows(y, whl_ref, rb_ref, idx_ref, wts_ref, rows)


def matmul_residual_layernorm_route(a, w, x, gain, bias, router_whl, router_bias, *, tm=LN_TM):
    m, kdim = a.shape
    d = w.shape[1]
    tm = min(tm, m)
    assert m % tm == 0
    const = lambda i: (0, 0)
    row = lambda i: (i, 0)
    return pl.pallas_call(
        _mm_res_ln_kernel,
        grid=(m // tm,),
        in_specs=[pl.BlockSpec((tm, kdim), row),
                  pl.BlockSpec((kdim, d), const, pipeline_mode=pl.Buffered(1)),
                  pl.BlockSpec((tm, d), row),
                  pl.BlockSpec((1, d), const),
                  pl.BlockSpec((1, d), const),
                  pl.BlockSpec((d, 2 * ROUTER_LANES), const),
                  pl.BlockSpec((1, ROUTER_LANES), const)],
        out_specs=[pl.BlockSpec((tm, d), row), pl.BlockSpec((tm, TOP_K), row), pl.BlockSpec((tm, TOP_K), row)],
        out_shape=[jax.ShapeDtypeStruct((m, d), F32), jax.ShapeDtypeStruct((m, TOP_K), jnp.int32),
                   jax.ShapeDtypeStruct((m, TOP_K), F32)],
        compiler_params=_params(("parallel",)),
        name="outproj_residual_ln_route",
    )(a, w, x, gain.reshape(1, d), bias.reshape(1, d), router_whl, router_bias)


def _attention_groups(rows):
    kh = min(NA_MAX_KH, rows)
    g = min(NA_ROW_GROUP, rows)
    assert rows % g == 0
    row_start = lambda r: min(max(r - kh // 2, 0), rows - kh)
    groups, specs = [], []
    for r0 in range(0, rows, g):
        lo = row_start(r0)
        n = row_start(r0 + g - 1) + kh - lo
        n = min(-(-n // NA_KEY_ROW_TILE) * NA_KEY_ROW_TILE, rows)
        lo = min(lo, rows - n)
        spec = (n, tuple((r0 + j - lo, row_start(r0 + j) - lo) for j in range(g)))
        if spec not in specs:
            specs.append(spec)
        groups.append((r0, lo, n, specs.index(spec)))
    return g, groups, specs


def _attention_bias_tables(rpb, rows):
    kh = min(NA_MAX_KH, rows)
    cols = jnp.arange(GRID_W)
    col_start = jnp.clip(cols - NA_KW // 2, 0, GRID_W - NA_KW)
    col_in_win = (cols[None, :] >= col_start[:, None]) & (cols[None, :] < col_start[:, None] + NA_KW)
    dc = jnp.clip(cols[None, :] - cols[:, None], -(NA_KW - 1), NA_KW - 1) + NA_KW - 1
    rpb = rpb.astype(F32)
    nh, n_dr, n_dc = rpb.shape
    cexp = jnp.zeros((nh, GRID_W, n_dr, GRID_W), F32)
    for j in range(n_dc):
        cexp = jnp.where((dc == j)[None, :, None, :], rpb[:, None, :, j, None], cexp)
    cexp = jnp.where(col_in_win[None, :, None, :], cexp, -jnp.inf).reshape(nh, GRID_W, n_dr * GRID_W)
    tables = []
    for n, per_row in _attention_groups(rows)[2]:
        blocks = []
        for q_off, win_off in per_row:
            first = (win_off - q_off + NA_MAX_KH - 1) * GRID_W
            window = cexp[:, :, first:first + kh * GRID_W]
            blocks.append(jnp.pad(window, ((0, 0), (0, 0), (win_off * GRID_W, (n - win_off - kh) * GRID_W)),
                                  constant_values=-jnp.inf))
        tables.append(jnp.concatenate(blocks, axis=1))
    return tables


def _attention_kernel(q_ref, k_ref, v_ref, *refs, g, groups):
    bias_refs, o_ref = refs[:-1], refs[-1]
    scale = NA_HEAD_DIM ** -0.5
    for r0, lo, n, tid in groups:
        qs = slice(r0 * GRID_W, (r0 + g) * GRID_W)
        ks = slice(lo * GRID_W, (lo + n) * GRID_W)
        s = lax.dot_general(q_ref[0, qs, :], k_ref[0, ks, :], (((1,), (1,)), ((), ())),
                            preferred_element_type=F32)
        s = s * scale + bias_refs[tid][0]
        m = jnp.max(s, axis=-1, keepdims=True)
        p = jnp.exp(s - m)
        l = jnp.sum(p, axis=-1, keepdims=True)
        o = jnp.dot(p.astype(BF16), v_ref[0, ks, :], preferred_element_type=F32) / l
        o_ref[0, qs, :] = o.astype(o_ref.dtype)


def neighbourhood_attention(qkv, bias_tables):
    b, s, d3 = qkv.shape
    d = d3 // 3
    nh = NA_HEADS
    hd = NA_HEAD_DIM
    g, groups, _ = _attention_groups(s // GRID_W)
    bias_specs = [pl.BlockSpec((1,) + t.shape[1:], lambda h, i: (h, 0, 0)) for t in bias_tables]
    return pl.pallas_call(
        functools.partial(_attention_kernel, g=g, groups=groups),
        grid=(nh, b),
        in_specs=[pl.BlockSpec((1, s, hd), lambda h, i: (i, 0, h)),
                  pl.BlockSpec((1, s, hd), lambda h, i: (i, 0, nh + h)),
                  pl.BlockSpec((1, s, hd), lambda h, i: (i, 0, 2 * nh + h))] + bias_specs,
        out_specs=pl.BlockSpec((1, s, hd), lambda h, i: (i, 0, h)),
        out_shape=jax.ShapeDtypeStruct((b, s, d), BF16),
        compiler_params=_params(("parallel", "arbitrary")),
        name="neighbourhood_attention",
    )(qkv, qkv, qkv, *bias_tables)


def _retention_kernel(lg_ref, q_ref, k_ref, v_ref, g_ref, o_ref, tb_ref, *, seq, chunk):
    h = pl.program_id(1)
    lg_f = lg_ref[0, h]
    lg_b = lg_ref[1, h]
    n_chunks = seq // chunk

    ia = lax.broadcasted_iota(jnp.int32, (chunk, chunk), 0)
    ic = lax.broadcasted_iota(jnp.int32, (chunk, chunk), 1)
    diff = (ia - ic).astype(F32)
    dmat = jnp.where(diff >= 0, jnp.exp(lg_f * jnp.maximum(diff, 0.0)), jnp.exp(lg_b * jnp.maximum(-diff, 0.0)))
    idx = lax.broadcasted_iota(jnp.int32, (chunk, 1), 0).astype(F32)
    qdec_f = jnp.exp(lg_f * (idx + 1.0))
    kdec_f = jnp.exp(lg_f * (chunk - 1.0 - idx))
    qdec_b = jnp.exp(lg_b * (chunk - idx))
    kdec_b = jnp.exp(lg_b * idx)
    cdec_f = jnp.exp(lg_f * chunk)
    cdec_b = jnp.exp(lg_b * chunk)

    def kv_outer(i, kdec):
        ks = (k_ref[0, i * chunk:(i + 1) * chunk, :].astype(F32) * kdec).astype(BF16)
        vi = v_ref[0, i * chunk:(i + 1) * chunk, :]
        return lax.dot_general(ks, vi, (((0,), (0,)), ((), ())), preferred_element_type=F32)

    state = jnp.zeros((RET_QK_DIM, RET_V_DIM), F32)
    for i in range(n_chunks - 1, -1, -1):
        tb_ref[i] = state.astype(BF16)
        if i > 0:
            state = state * cdec_b + kv_outer(i, kdec_b)

    state = jnp.zeros((RET_QK_DIM, RET_V_DIM), F32)
    for i in range(n_chunks):
        sl = slice(i * chunk, (i + 1) * chunk)
        qi = q_ref[0, sl, :]
        ki = k_ref[0, sl, :]
        vi = v_ref[0, sl, :]
        s = lax.dot_general(qi, ki, (((1,), (1,)), ((), ())), preferred_element_type=F32)
        y = jnp.dot((s * dmat).astype(BF16), vi, preferred_element_type=F32)
        y = y + jnp.dot(qi, tb_ref[i], preferred_element_type=F32) * qdec_b
        if i > 0:
            y = y + jnp.dot(qi, state.astype(BF16), preferred_element_type=F32) * qdec_f
        if i < n_chunks - 1:
            state = state * cdec_f + kv_outer(i, kdec_f)
        y = y * lax.rsqrt(jnp.mean(y * y, axis=-1, keepdims=True) + GN_EPS)
        o_ref[0, sl, :] = (g_ref[0, sl, :].astype(F32) * y).astype(o_ref.dtype)


def retention(qk, v, g, lg):
    b, s, _ = qk.shape
    nh = RET_HEADS
    dk, dv = RET_QK_DIM, RET_V_DIM
    chunk = min(RET_CHUNK, s)
    assert s % chunk == 0
    return pl.pallas_call(
        functools.partial(_retention_kernel, seq=s, chunk=chunk),
        grid=(b, nh),
        in_specs=[pl.BlockSpec(memory_space=pltpu.SMEM),
                  pl.BlockSpec((1, s, dk), lambda i, h: (i, 0, h)),
                  pl.BlockSpec((1, s, dk), lambda i, h: (i, 0, nh + h)),
                  pl.BlockSpec((1, s, dv), lambda i, h: (i, 0, h)),
                  pl.BlockSpec((1, s, dv), lambda i, h: (i, 0, h))],
        out_specs=pl.BlockSpec((1, s, dv), lambda i, h: (i, 0, h)),
        out_shape=jax.ShapeDtypeStruct((b, s, nh * dv), BF16),
        scratch_shapes=[pltpu.VMEM((s // chunk, dk, dv), BF16)],
        compiler_params=_params(("parallel", "arbitrary")),
        name="retention",
    )(lg, qk, qk, v, g)


def _route_rows(x, whl_ref, b_ref, idx_ref, wts_ref, rows):
    xh = x.astype(BF16)
    xl = (x - xh.astype(F32)).astype(BF16)
    both = jnp.dot(xh, whl_ref[...], preferred_element_type=F32)
    logits = (both[:, :ROUTER_LANES] + both[:, ROUTER_LANES:]
              + jnp.dot(xl, whl_ref[:, :ROUTER_LANES], preferred_element_type=F32)) + b_ref[...]
    lane = lax.broadcasted_iota(jnp.int32, logits.shape, 1)
    neg = -jnp.inf
    big = jnp.int32(ROUTER_LANES)

    gl = jnp.where(lane < N_GROUPS, logits, neg)
    gmax = jnp.max(gl, axis=-1, keepdims=True)
    g_sel = jnp.min(jnp.where(gl == gmax, lane, big), axis=-1, keepdims=True)
    w_grp = 1.0 / jnp.sum(jnp.exp(gl - gmax), axis=-1, keepdims=True)

    lo = N_GROUPS + EXPERTS_PER_GROUP * g_sel
    el = jnp.where((lane >= lo) & (lane < lo + EXPERTS_PER_GROUP), logits, neg)
    v1 = jnp.max(el, axis=-1, keepdims=True)
    i1 = jnp.min(jnp.where(el == v1, lane, big), axis=-1, keepdims=True)
    el2 = jnp.where(lane == i1, neg, el)
    v2 = jnp.max(el2, axis=-1, keepdims=True)
    i2 = jnp.min(jnp.where(el2 == v2, lane, big), axis=-1, keepdims=True)
    e21 = jnp.exp(v2 - v1)
    p1 = 1.0 / (1.0 + e21)
    p2 = e21 * p1
    idx_ref[rows, 0:1] = i1 - N_GROUPS
    idx_ref[rows, 1:2] = i2 - N_GROUPS
    wts_ref[rows, 0:1] = p1 * w_grp
    wts_ref[rows, 1:2] = p2 * w_grp


def _router_operands(w_group, b_group, w_expert, b_expert):
    d = w_group.shape[0]
    w = jnp.concatenate([w_group.astype(F32), w_expert.astype(F32).reshape(d, N_EXPERTS)], axis=1)
    w = jnp.pad(w, ((0, 0), (0, ROUTER_LANES - w.shape[1])))
    wh = w.astype(BF16)
    wl = (w - wh.astype(F32)).astype(BF16)
    bias = jnp.concatenate([b_group.astype(F32), b_expert.astype(F32).reshape(N_EXPERTS)])
    bias = jnp.pad(bias, (0, ROUTER_LANES - bias.shape[0])).reshape(1, ROUTER_LANES)
    return jnp.concatenate([wh, wl], axis=1), bias


def _dispatch_kernel(pos_ref, pad_lo_ref, pad_hi_ref, tile_valid_ref, x_ref, xs_hbm, sem, pad_sem, tile_sem, *, tm):
    t = pl.program_id(0)

    def row_copy(src_row, dst_row, s):
        return pltpu.make_async_copy(x_ref.at[pl.ds(src_row, 1), :], xs_hbm.at[pl.ds(dst_row, 1), :], s)

    def wait_rows(n, s):
        pltpu.make_async_copy(x_ref.at[pl.ds(0, n), :], xs_hbm.at[pl.ds(0, n), :], s).wait()

    for i in range(tm):
        for k in range(TOP_K):
            row_copy(i, pos_ref[(t * tm + i) * TOP_K + k], sem).start()

    @pl.when(t == 0)
    def _():
        def tile_body(j, n_started):
            @pl.when(tile_valid_ref[j] == 0)
            def _():
                dst = xs_hbm.at[pl.ds(pl.multiple_of(j * tm, tm), tm), :]
                pltpu.make_async_copy(x_ref, dst, tile_sem).start()
            return n_started + (1 - tile_valid_ref[j])
        n_tile_copies = lax.fori_loop(0, tile_valid_ref.shape[0], tile_body, 0)

        def expert_body(e, total):
            def pad_issue(r, cc):
                row_copy(0, r, pad_sem).start()
                return cc
            lax.fori_loop(pad_lo_ref[e], pad_hi_ref[e], pad_issue, 0)
            return total + (pad_hi_ref[e] - pad_lo_ref[e])
        n_pad = lax.fori_loop(0, N_EXPERTS, expert_body, 0)

        def wait_each(n, s):
            def body(i, carry):
                wait_rows(n, s)
                return carry
            return body
        lax.fori_loop(0, n_pad, wait_each(1, pad_sem), 0)
        lax.fori_loop(0, n_tile_copies, wait_each(tm, tile_sem), 0)

    for _ in range(TOP_K):
        wait_rows(tm, sem)


def dispatch_rows(x, pos, pad_lo, pad_hi, tile_valid, tm):
    n, d = x.shape
    assert n % tm == 0
    grid_spec = pltpu.PrefetchScalarGridSpec(
        num_scalar_prefetch=4,
        grid=(n // tm,),
        in_specs=[pl.BlockSpec((tm, d), lambda t, *_: (t, 0))],
        out_specs=pl.BlockSpec(memory_space=pl.ANY),
        scratch_shapes=[pltpu.SemaphoreType.DMA] * 3,
    )
    return pl.pallas_call(
        functools.partial(_dispatch_kernel, tm=tm),
        grid_spec=grid_spec,
        out_shape=jax.ShapeDtypeStruct((tile_valid.shape[0] * tm, d), x.dtype),
        compiler_params=_params(("arbitrary",)),
        name="dispatch_rows",
    )(pos, pad_lo, pad_hi, tile_valid, x)


def _routing_metadata(expert_idx, tm):
    n = expert_idx.shape[0]
    na = n * TOP_K
    n_tiles = na // tm + N_EXPERTS
    e_flat = expert_idx.reshape(na)
    onehot = (e_flat[:, None] == jnp.arange(N_EXPERTS, dtype=jnp.int32)[None, :]).astype(jnp.int32)
    csum = jnp.cumsum(onehot, axis=0)
    counts = csum[-1]
    rank = jnp.sum(onehot * csum, axis=1) - 1
    padded = ((counts + tm - 1) // tm) * tm
    ends = jnp.cumsum(padded)
    starts = ends - padded
    pos = (starts[e_flat] + rank).astype(jnp.int32)
    tile_start = jnp.arange(n_tiles, dtype=jnp.int32) * tm
    tile_expert = jnp.sum((tile_start[:, None] >= ends[None, :]).astype(jnp.int32), axis=1)
    tile_valid = (tile_start < ends[-1]).astype(jnp.int32)
    last_expert = jnp.max(jnp.where(counts > 0, jnp.arange(N_EXPERTS, dtype=jnp.int32), 0))
    tile_expert = jnp.where(tile_valid > 0, tile_expert, last_expert)
    pad_lo = (starts + counts).astype(jnp.int32)
    return pos, pad_lo, ends.astype(jnp.int32), tile_expert, tile_valid


def _row_gather_start(src_hbm, row_of, base, dst, sem, n_rows, *, unrolled=False):
    def start(r):
        pltpu.make_async_copy(src_hbm.at[pl.ds(row_of(base + r), 1), :], dst.at[pl.ds(r, 1), :], sem).start()

    if unrolled:
        for r in range(n_rows):
            start(r)
    else:
        def body(r, carry):
            start(r)
            return carry
        lax.fori_loop(0, n_rows, body, 0, unroll=8)


def _row_gather_wait(src_hbm, dst, sem, n_rows):
    pltpu.make_async_copy(src_hbm.at[pl.ds(0, n_rows), :], dst, sem).wait()


def _grouped_swiglu_kernel(tile_expert_ref, tile_valid_ref, x_ref, wg_ref, wu_ref, wd_ref, o_ref, wgb, wub, wdb):
    t = pl.program_id(0)
    prev = jnp.maximum(t - 1, 0)
    valid = tile_valid_ref[t] > 0

    @pl.when(valid)
    def _():
        @pl.when((t == 0) | (tile_expert_ref[t] != tile_expert_ref[prev]))
        def _():
            wgb[...] = wg_ref[0].astype(BF16)
            wub[...] = wu_ref[0].astype(BF16)
            wdb[...] = wd_ref[0].astype(BF16)

        xb = x_ref[...].astype(BF16)
        hg = jnp.dot(xb, wgb[...], preferred_element_type=F32)
        hu = jnp.dot(xb, wub[...], preferred_element_type=F32)
        hcur = (hg / (1.0 + jnp.exp(-hg)) * hu).astype(BF16)
        o_ref[...] = jnp.dot(hcur, wdb[...], preferred_element_type=F32)

    @pl.when(jnp.logical_not(valid))
    def _():
        o_ref[...] = jnp.zeros_like(o_ref)


def grouped_swiglu(x_sorted, tile_expert, tile_valid, w_gate, w_up, w_down, *, tm=MOE_TM):
    d, f = w_gate.shape[-2:]
    n_tiles = tile_expert.shape[0]
    grid_spec = pltpu.PrefetchScalarGridSpec(
        num_scalar_prefetch=2,
        grid=(n_tiles,),
        in_specs=[pl.BlockSpec((tm, d), lambda t, te, tv: (t, 0)),
                  pl.BlockSpec((1, d, f), lambda t, te, tv: (te[t], 0, 0)),
                  pl.BlockSpec((1, d, f), lambda t, te, tv: (te[t], 0, 0)),
                  pl.BlockSpec((1, f, d), lambda t, te, tv: (te[t], 0, 0))],
        out_specs=pl.BlockSpec((tm, d), lambda t, te, tv: (t, 0)),
        scratch_shapes=[pltpu.VMEM((d, f), BF16), pltpu.VMEM((d, f), BF16), pltpu.VMEM((f, d), BF16)],
    )
    return pl.pallas_call(
        _grouped_swiglu_kernel,
        grid_spec=grid_spec,
        out_shape=jax.ShapeDtypeStruct((n_tiles * tm, d), F32),
        compiler_params=_params(("arbitrary",)),
        name="grouped_swiglu",
    )(tile_expert, tile_valid, x_sorted, w_gate, w_up, w_down)


def _combine_ln_kernel(pos_ref, rows_hbm, x_ref, wts_ref, g_ref, b_ref, *refs, tm):
    out_refs, (buf0, buf1, sem) = refs[:-3], refs[-3:]
    t = pl.program_id(0)
    slot = lax.rem(t, 2)

    def start(tile, sl, unrolled):
        _row_gather_start(rows_hbm, lambda a: pos_ref[2 * a], tile * tm, buf0.at[sl], sem.at[0, sl], tm,
                          unrolled=unrolled)
        _row_gather_start(rows_hbm, lambda a: pos_ref[2 * a + 1], tile * tm, buf1.at[sl], sem.at[1, sl], tm,
                          unrolled=unrolled)

    @pl.when(t == 0)
    def _():
        start(0, 0, False)

    _row_gather_wait(rows_hbm, buf0.at[slot], sem.at[0, slot], tm)
    _row_gather_wait(rows_hbm, buf1.at[slot], sem.at[1, slot], tm)

    @pl.when(t + 1 < pl.num_programs(0))
    def _():
        start(t + 1, 1 - slot, True)

    w = wts_ref[...]
    z = DEEPNORM_ALPHA * x_ref[...] + w[:, 0:1] * buf0[slot] + w[:, 1:2] * buf1[slot]
    y = _layer_norm_rows(z, g_ref[...], b_ref[...])
    for o_ref in out_refs:
        o_ref[...] = y.astype(o_ref.dtype)


def combine_residual_layernorm(rows, pos, x, wts, gain, bias, *, out_dtypes, tm=COMBINE_TM):
    n, d = x.shape
    tm = min(tm, n)
    grid_spec = pltpu.PrefetchScalarGridSpec(
        num_scalar_prefetch=1,
        grid=(n // tm,),
        in_specs=[pl.BlockSpec(memory_space=pl.ANY),
                  pl.BlockSpec((tm, d), lambda t, p: (t, 0)),
                  pl.BlockSpec((tm, TOP_K), lambda t, p: (t, 0)),
                  pl.BlockSpec((1, d), lambda t, p: (0, 0)),
                  pl.BlockSpec((1, d), lambda t, p: (0, 0))],
        out_specs=[pl.BlockSpec((tm, d), lambda t, p: (t, 0)) for _ in out_dtypes],
        scratch_shapes=[pltpu.VMEM((2, tm, d), F32), pltpu.VMEM((2, tm, d), F32),
                        pltpu.SemaphoreType.DMA((2, 2))],
    )
    return pl.pallas_call(
        functools.partial(_combine_ln_kernel, tm=tm),
        grid_spec=grid_spec,
        out_shape=[jax.ShapeDtypeStruct((n, d), dt) for dt in out_dtypes],
        compiler_params=_params(("arbitrary",)),
        name="moe_combine_ln",
    )(pos, rows, x, wts, gain.reshape(1, d), bias.reshape(1, d))


def moe_block(x, expert_idx, wts, w_gate, w_up, w_down, layer, gain, bias, *, out_dtypes, tm=MOE_TM):
    d, f = w_gate.shape[-2:]
    pos, pad_lo, pad_hi, tile_expert, tile_valid = _routing_metadata(expert_idx, tm)
    x_sorted = dispatch_rows(x, pos, pad_lo, pad_hi, tile_valid, tm)
    rows = grouped_swiglu(x_sorted, tile_expert + layer * N_EXPERTS, tile_valid,
                          w_gate.reshape(-1, d, f), w_up.reshape(-1, d, f), w_down.reshape(-1, f, d), tm=tm)
    return combine_residual_layernorm(rows, pos, x, wts, gain, bias, out_dtypes=out_dtypes)


def _rotary_tables(seq):
    half = RET_QK_DIM // 2
    freqs = ROPE_BASE ** (-jnp.arange(half, dtype=F32) / half)
    ang = jnp.arange(seq).astype(F32)[:, None] * freqs[None, :]
    return jnp.cos(ang), jnp.sin(ang)


def kernel(x, attn_w_in, attn_w_out, attn_rpb, ret_w_in, ret_w_out, ret_decay_fwd, ret_decay_bwd,
           moe_w_group_router, moe_b_group_router, moe_w_expert_router, moe_b_expert_router,
           moe_w_gate, moe_w_up, moe_w_down, ln_gain, ln_bias):
    b, s, d = x.shape
    n = b * s
    xf = x.reshape(n, d).astype(F32)
    xb = xf
    depth = ln_gain.shape[0]
    for i in range(depth):
        j = i // 2
        if i % 2 == 0:
            qkv = matmul(xb, attn_w_in[j].astype(F32))
            bias_tables = _attention_bias_tables(attn_rpb[j], s // GRID_W)
            mixed = neighbourhood_attention(qkv.reshape(b, s, 3 * d), bias_tables).reshape(n, d)
            w_out = attn_w_out[j]
        else:
            w_in = ret_w_in[j].astype(F32)
            qk_w, v_w = 2 * RET_HEADS * RET_QK_DIM, RET_HEADS * RET_V_DIM
            qk = matmul(xb, w_in, col_start=0, n_cols=qk_w, epilogue="rotary", rotary=_rotary_tables(s),
                        scale_from_col=qk_w // 2)
            v = matmul(xb, w_in, col_start=qk_w, n_cols=v_w)
            g = matmul(xb, w_in, col_start=qk_w + v_w, n_cols=v_w, epilogue="silu")
            lg = jnp.stack([jnp.log1p(-jnp.exp(ret_decay_fwd[j].astype(F32))),
                            jnp.log1p(-jnp.exp(ret_decay_bwd[j].astype(F32)))])
            mixed = retention(qk.reshape(b, s, qk_w), v.reshape(b, s, v_w), g.reshape(b, s, v_w),
                              lg).reshape(n, v_w)
            w_out = ret_w_out[j]
        whl, rbias = _router_operands(moe_w_group_router[i], moe_b_group_router[i], moe_w_expert_router[i],
                                      moe_b_expert_router[i])
        xf, expert_idx, wts = matmul_residual_layernorm_route(mixed, w_out.astype(BF16), xf, ln_gain[i, 0],
                                                              ln_bias[i, 0], whl, rbias)
        last = i == depth - 1
        outs = moe_block(xf, expert_idx, wts, moe_w_gate, moe_w_up, moe_w_down, i,
                         ln_gain[i, 1], ln_bias[i, 1], out_dtypes=(F32,) if last else (F32, BF16))
        xf, xb = outs[0], outs[-1]
    return xf.reshape(b, s, d).astype(x.dtype)
```

```python
import functools
import math

import jax
import jax.numpy as jnp
from jax import lax
from jax.experimental import pallas as pl
from jax.experimental.pallas import tpu as pltpu

GRID_W = 64
NA_HEADS = 16
NA_HEAD_DIM = 128
NA_MAX_KH = 8
NA_KW = 16
RET_HEADS = 8
RET_QK_DIM = 256
RET_V_DIM = 512
ROPE_BASE = 10000.0
N_GROUPS = 4
EXPERTS_PER_GROUP = 8
N_EXPERTS = N_GROUPS * EXPERTS_PER_GROUP
TOP_K = 2
DEPTH = 2
DEEPNORM_ALPHA = (2 * DEPTH) ** 0.25
LN_EPS = 1e-5
GN_EPS = 1e-6

V7X_LANES = 128
V7X_VMEM_LIMIT_BYTES = 56 * 1024 * 1024

BF16 = jnp.bfloat16
F32 = jnp.float32

MM_TM = 1024
MM_TN = 1024
LN_TM = 512
MOE_TM = 256
COMBINE_TM = 256
RET_CHUNK = 256
NA_ROW_GROUP = 4
NA_KEY_ROW_TILE = 4
ROUTER_LANES = V7X_LANES


def _params(semantics):
    return pltpu.CompilerParams(dimension_semantics=semantics, vmem_limit_bytes=V7X_VMEM_LIMIT_BYTES)


def _matmul_kernel(x_ref, w_ref, *refs, epilogue, scale_from_tile):
    o_ref, wb_ref = refs[-2:]

    @pl.when(pl.program_id(1) == 0)
    def _():
        wb_ref[...] = w_ref[...].astype(BF16)

    acc = jnp.dot(x_ref[...].astype(BF16), wb_ref[...], preferred_element_type=F32)
    if epilogue == "rotary":
        cos, sin = refs[0][...], refs[1][...]
        scale = jnp.where(pl.program_id(0) >= scale_from_tile, RET_QK_DIM ** -0.5, 1.0).astype(F32)
        half = RET_QK_DIM // 2
        for c0 in range(0, acc.shape[1], RET_QK_DIM):
            t1 = acc[:, c0:c0 + half]
            t2 = acc[:, c0 + half:c0 + RET_QK_DIM]
            o_ref[:, c0:c0 + half] = ((t1 * cos - t2 * sin) * scale).astype(o_ref.dtype)
            o_ref[:, c0 + half:c0 + RET_QK_DIM] = ((t1 * sin + t2 * cos) * scale).astype(o_ref.dtype)
    elif epilogue == "silu":
        o_ref[...] = (acc / (1.0 + jnp.exp(-acc))).astype(o_ref.dtype)
    else:
        o_ref[...] = acc.astype(o_ref.dtype)


def matmul(x, w, *, col_start=0, n_cols=None, epilogue=None, rotary=None, scale_from_col=None,
           tm=MM_TM, tn=MM_TN):
    m, k = x.shape
    n = w.shape[1] - col_start if n_cols is None else n_cols
    tm, tn = min(tm, m), min(tn, n)
    assert m % tm == 0 and n % tn == 0 and col_start % tn == 0
    tile0 = col_start // tn
    in_specs = [pl.BlockSpec((tm, k), lambda j, i: (i, 0)),
                pl.BlockSpec((k, tn), lambda j, i: (0, tile0 + j))]
    operands = [x, w]
    scale_from_tile = None
    if epilogue == "rotary":
        seq = rotary[0].shape[0]
        assert seq % tm == 0 and tn % RET_QK_DIM == 0 and scale_from_col % tn == 0
        in_specs += [pl.BlockSpec((tm, RET_QK_DIM // 2), lambda j, i: (i % (seq // tm), 0))] * 2
        operands += list(rotary)
        scale_from_tile = scale_from_col // tn
    return pl.pallas_call(
        functools.partial(_matmul_kernel, epilogue=epilogue, scale_from_tile=scale_from_tile),
        grid=(n // tn, m // tm),
        in_specs=in_specs,
        out_specs=pl.BlockSpec((tm, tn), lambda j, i: (i, j)),
        out_shape=jax.ShapeDtypeStruct((m, n), BF16),
        scratch_shapes=[pltpu.VMEM((k, tn), BF16)],
        compiler_params=_params(("parallel", "arbitrary")),
        name="proj_matmul",
    )(*operands)


def _layer_norm_rows(z, gain, bias):
    mu = jnp.mean(z, axis=-1, keepdims=True)
    zc = z - mu
    var = jnp.mean(zc * zc, axis=-1, keepdims=True)
    return zc * lax.rsqrt(var + LN_EPS) * gain + bias


LN_ROW_SPLIT = 2


def _mm_res_ln_kernel(a_ref, w_ref, x_ref, g_ref, b_ref, whl_ref, rb_ref, o_ref, idx_ref, wts_ref):
    hm = a_ref.shape[0] // LN_ROW_SPLIT
    for h in range(LN_ROW_SPLIT):
        rows = slice(h * hm, (h + 1) * hm)
        z = DEEPNORM_ALPHA * x_ref[rows, :] + jnp.dot(a_ref[rows, :], w_ref[...], preferred_element_type=F32)
        y = _layer_norm_rows(z, g_ref[...], b_ref[...])
        o_ref[rows, :] = y
        _route_rows(y, whl_ref, rb_ref, idx_ref, wts_ref, rows)


def matmul_residual_layernorm_route(a, w, x, gain, bias, router_whl, router_bias, *, tm=LN_TM):
    m, kdim = a.shape
    d = w.shape[1]
    tm = min(tm, m)
    assert m % tm == 0
    const = lambda i: (0, 0)
    row = lambda i: (i, 0)
    return pl.pallas_call(
        _mm_res_ln_kernel,
        grid=(m // tm,),
        in_specs=[pl.BlockSpec((tm, kdim), row),
                  pl.BlockSpec((kdim, d), const, pipeline_mode=pl.Buffered(1)),
                  pl.BlockSpec((tm, d), row),
                  pl.BlockSpec((1, d), const),
                  pl.BlockSpec((1, d), const),
                  pl.BlockSpec((d, 2 * ROUTER_LANES), const),
                  pl.BlockSpec((1, ROUTER_LANES), const)],
        out_specs=[pl.BlockSpec((tm, d), row), pl.BlockSpec((tm, TOP_K), row), pl.BlockSpec((tm, TOP_K), row)],
        out_shape=[jax.ShapeDtypeStruct((m, d), F32), jax.ShapeDtypeStruct((m, TOP_K), jnp.int32),
                   jax.ShapeDtypeStruct((m, TOP_K), F32)],
        compiler_params=_params(("parallel",)),
        name="outproj_residual_ln_route",
    )(a, w, x, gain.reshape(1, d), bias.reshape(1, d), router_whl, router_bias)


def _attention_groups(rows):
    kh = min(NA_MAX_KH, rows)
    g = min(NA_ROW_GROUP, rows)
    assert rows % g == 0
    row_start = lambda r: min(max(r - kh // 2, 0), rows - kh)
    groups, specs = [], []
    for r0 in range(0, rows, g):
        lo = row_start(r0)
        n = row_start(r0 + g - 1) + kh - lo
        n = min(-(-n // NA_KEY_ROW_TILE) * NA_KEY_ROW_TILE, rows)
        lo = min(lo, rows - n)
        spec = (n, tuple((r0 + j - lo, row_start(r0 + j) - lo) for j in range(g)))
        if spec not in specs:
            specs.append(spec)
        groups.append((r0, lo, n, specs.index(spec)))
    return g, groups, specs


def _attention_bias_tables(rpb, rows):
    kh = min(NA_MAX_KH, rows)
    cols = jnp.arange(GRID_W)
    col_start = jnp.clip(cols - NA_KW // 2, 0, GRID_W - NA_KW)
    col_in_win = (cols[None, :] >= col_start[:, None]) & (cols[None, :] < col_start[:, None] + NA_KW)
    dc = jnp.clip(cols[None, :] - cols[:, None], -(NA_KW - 1), NA_KW - 1) + NA_KW - 1
    rpb = rpb.astype(F32)
    nh, n_dr, n_dc = rpb.shape
    cexp = jnp.zeros((nh, GRID_W, n_dr, GRID_W), F32)
    for j in range(n_dc):
        cexp = jnp.where((dc == j)[None, :, None, :], rpb[:, None, :, j, None], cexp)
    cexp = jnp.where(col_in_win[None, :, None, :], cexp, -jnp.inf).reshape(nh, GRID_W, n_dr * GRID_W)
    tables = []
    for n, per_row in _attention_groups(rows)[2]:
        blocks = []
        for q_off, win_off in per_row:
            first = (win_off - q_off + NA_MAX_KH - 1) * GRID_W
            window = cexp[:, :, first:first + kh * GRID_W]
            blocks.append(jnp.pad(window, ((0, 0), (0, 0), (win_off * GRID_W, (n - win_off - kh) * GRID_W)),
                                  constant_values=-jnp.inf))
        tables.append(jnp.concatenate(blocks, axis=1))
    return tables


def _attention_kernel(q_ref, k_ref, v_ref, *refs, g, groups):
    bias_refs, o_ref = refs[:-1], refs[-1]
    scale = NA_HEAD_DIM ** -0.5
    for r0, lo, n, tid in groups:
        qs = slice(r0 * GRID_W, (r0 + g) * GRID_W)
        ks = slice(lo * GRID_W, (lo + n) * GRID_W)
        s = lax.dot_general(q_ref[0, qs, :], k_ref[0, ks, :], (((1,), (1,)), ((), ())),
                            preferred_element_type=F32)
        s = s * scale + bias_refs[tid][0]
        m = jnp.max(s, axis=-1, keepdims=True)
        p = jnp.exp(s - m)
        l = jnp.sum(p, axis=-1, keepdims=True)
        o = jnp.dot(p.astype(BF16), v_ref[0, ks, :], preferred_element_type=F32) / l
        o_ref[0, qs, :] = o.astype(o_ref.dtype)


def neighbourhood_attention(qkv, bias_tables):
    b, s, d3 = qkv.shape
    d = d3 // 3
    nh = NA_HEADS
    hd = NA_HEAD_DIM
    g, groups, _ = _attention_groups(s // GRID_W)
    bias_specs = [pl.BlockSpec((1,) + t.shape[1:], lambda h, i: (h, 0, 0)) for t in bias_tables]
    return pl.pallas_call(
        functools.partial(_attention_kernel, g=g, groups=groups),
        grid=(nh, b),
        in_specs=[pl.BlockSpec((1, s, hd), lambda h, i: (i, 0, h)),
                  pl.BlockSpec((1, s, hd), lambda h, i: (i, 0, nh + h)),
                  pl.BlockSpec((1, s, hd), lambda h, i: (i, 0, 2 * nh + h))] + bias_specs,
        out_specs=pl.BlockSpec((1, s, hd), lambda h, i: (i, 0, h)),
        out_shape=jax.ShapeDtypeStruct((b, s, d), BF16),
        compiler_params=_params(("parallel", "arbitrary")),
        name="neighbourhood_attention",
    )(qkv, qkv, qkv, *bias_tables)


def _retention_kernel(lg_ref, q_ref, k_ref, v_ref, g_ref, o_ref, tb_ref, *, seq, chunk):
    h = pl.program_id(1)
    lg_f = lg_ref[0, h]
    lg_b = lg_ref[1, h]
    n_chunks = seq // chunk

    ia = lax.broadcasted_iota(jnp.int32, (chunk, chunk), 0)
    ic = lax.broadcasted_iota(jnp.int32, (chunk, chunk), 1)
    diff = (ia - ic).astype(F32)
    dmat = jnp.where(diff >= 0, jnp.exp(lg_f * jnp.maximum(diff, 0.0)), jnp.exp(lg_b * jnp.maximum(-diff, 0.0)))
    idx = lax.broadcasted_iota(jnp.int32, (chunk, 1), 0).astype(F32)
    qdec_f = jnp.exp(lg_f * (idx + 1.0))
    kdec_f = jnp.exp(lg_f * (chunk - 1.0 - idx))
    qdec_b = jnp.exp(lg_b * (chunk - idx))
    kdec_b = jnp.exp(lg_b * idx)
    cdec_f = jnp.exp(lg_f * chunk)
    cdec_b = jnp.exp(lg_b * chunk)

    def kv_outer(i, kdec):
        ks = (k_ref[0, i * chunk:(i + 1) * chunk, :].astype(F32) * kdec).astype(BF16)
        vi = v_ref[0, i * chunk:(i + 1) * chunk, :]
        return lax.dot_general(ks, vi, (((0,), (0,)), ((), ())), preferred_element_type=F32)

    state = jnp.zeros((RET_QK_DIM, RET_V_DIM), F32)
    for i in range(n_chunks - 1, -1, -1):
        tb_ref[i] = state.astype(BF16)
        if i > 0:
            state = state * cdec_b + kv_outer(i, kdec_b)

    state = jnp.zeros((RET_QK_DIM, RET_V_DIM), F32)
    for i in range(n_chunks):
        sl = slice(i * chunk, (i + 1) * chunk)
        qi = q_ref[0, sl, :]
        ki = k_ref[0, sl, :]
        vi = v_ref[0, sl, :]
        s = lax.dot_general(qi, ki, (((1,), (1,)), ((), ())), preferred_element_type=F32)
        y = jnp.dot((s * dmat).astype(BF16), vi, preferred_element_type=F32)
        y = y + jnp.dot(qi, tb_ref[i], preferred_element_type=F32) * qdec_b
        if i > 0:
            y = y + jnp.dot(qi, state.astype(BF16), preferred_element_type=F32) * qdec_f
        if i < n_chunks - 1:
            state = state * cdec_f + kv_outer(i, kdec_f)
        y = y * lax.rsqrt(jnp.mean(y * y, axis=-1, keepdims=True) + GN_EPS)
        o_ref[0, sl, :] = (g_ref[0, sl, :].astype(F32) * y).astype(o_ref.dtype)


def retention(qk, v, g, lg):
    b, s, _ = qk.shape
    nh = RET_HEADS
    dk, dv = RET_QK_DIM, RET_V_DIM
    chunk = min(RET_CHUNK, s)
    assert s % chunk == 0
    return pl.pallas_call(
        functools.partial(_retention_kernel, seq=s, chunk=chunk),
        grid=(b, nh),
        in_specs=[pl.BlockSpec(memory_space=pltpu.SMEM),
                  pl.BlockSpec((1, s, dk), lambda i, h: (i, 0, h)),
                  pl.BlockSpec((1, s, dk), lambda i, h: (i, 0, nh + h)),
                  pl.BlockSpec((1, s, dv), lambda i, h: (i, 0, h)),
                  pl.BlockSpec((1, s, dv), lambda i, h: (i, 0, h))],
        out_specs=pl.BlockSpec((1, s, dv), lambda i, h: (i, 0, h)),
        out_shape=jax.ShapeDtypeStruct((b, s, nh * dv), BF16),
        scratch_shapes=[pltpu.VMEM((s // chunk, dk, dv), BF16)],
        compiler_params=_params(("parallel", "arbitrary")),
        name="retention",
    )(lg, qk, qk, v, g)


def _route_rows(x, whl_ref, b_ref, idx_ref, wts_ref, rows):
    xh = x.astype(BF16)
    xl = (x - xh.astype(F32)).astype(BF16)
    both = jnp.dot(xh, whl_ref[...], preferred_element_type=F32)
    logits = (both[:, :ROUTER_LANES] + both[:, ROUTER_LANES:]
              + jnp.dot(xl, whl_ref[:, :ROUTER_LANES], preferred_element_type=F32)) + b_ref[...]
    lane = lax.broadcasted_iota(jnp.int32, logits.shape, 1)
    neg = -jnp.inf
    big = jnp.int32(ROUTER_LANES)

    gl = jnp.where(lane < N_GROUPS, logits, neg)
    gmax = jnp.max(gl, axis=-1, keepdims=True)
    g_sel = jnp.min(jnp.where(gl == gmax, lane, big), axis=-1, keepdims=True)
    w_grp = 1.0 / jnp.sum(jnp.exp(gl - gmax), axis=-1, keepdims=True)

    lo = N_GROUPS + EXPERTS_PER_GROUP * g_sel
    el = jnp.where((lane >= lo) & (lane < lo + EXPERTS_PER_GROUP), logits, neg)
    v1 = jnp.max(el, axis=-1, keepdims=True)
    i1 = jnp.min(jnp.where(el == v1, lane, big), axis=-1, keepdims=True)
    el2 = jnp.where(lane == i1, neg, el)
    v2 = jnp.max(el2, axis=-1, keepdims=True)
    i2 = jnp.min(jnp.where(el2 == v2, lane, big), axis=-1, keepdims=True)
    e21 = jnp.exp(v2 - v1)
    p1 = 1.0 / (1.0 + e21)
    p2 = e21 * p1
    idx_ref[rows, 0:1] = i1 - N_GROUPS
    idx_ref[rows, 1:2] = i2 - N_GROUPS
    wts_ref[rows, 0:1] = p1 * w_grp
    wts_ref[rows, 1:2] = p2 * w_grp


def _router_operands(w_group, b_group, w_expert, b_expert):
    d = w_group.shape[0]
    w = jnp.concatenate([w_group.astype(F32), w_expert.astype(F32).reshape(d, N_EXPERTS)], axis=1)
    w = jnp.pad(w, ((0, 0), (0, ROUTER_LANES - w.shape[1])))
    wh = w.astype(BF16)
    wl = (w - wh.astype(F32)).astype(BF16)
    bias = jnp.concatenate([b_group.astype(F32), b_expert.astype(F32).reshape(N_EXPERTS)])
    bias = jnp.pad(bias, (0, ROUTER_LANES - bias.shape[0])).reshape(1, ROUTER_LANES)
    return jnp.concatenate([wh, wl], axis=1), bias


def _dispatch_kernel(pos_ref, tile_fill_ref, x_ref, xs_hbm, sem, fill_sem, *, tm):
    t = pl.program_id(0)

    @pl.when(t == 0)
    def _():
        def tile_body(j, n_started):
            @pl.when(tile_fill_ref[j] > 0)
            def _():
                dst = xs_hbm.at[pl.ds(pl.multiple_of(j * tm, tm), tm), :]
                pltpu.make_async_copy(x_ref, dst, fill_sem).start()
            return n_started + tile_fill_ref[j]
        n_fills = lax.fori_loop(0, tile_fill_ref.shape[0], tile_body, 0)

        def wait_fill(i, carry):
            pltpu.make_async_copy(x_ref, xs_hbm.at[pl.ds(0, tm), :], fill_sem).wait()
            return carry
        lax.fori_loop(0, n_fills, wait_fill, 0)

    for i in range(tm):
        for k in range(TOP_K):
            dst = xs_hbm.at[pl.ds(pos_ref[(t * tm + i) * TOP_K + k], 1), :]
            pltpu.make_async_copy(x_ref.at[pl.ds(i, 1), :], dst, sem).start()
    for _ in range(TOP_K):
        pltpu.make_async_copy(x_ref, xs_hbm.at[pl.ds(0, tm), :], sem).wait()


def dispatch_rows(x, pos, tile_fill, tm):
    n, d = x.shape
    assert n % tm == 0
    grid_spec = pltpu.PrefetchScalarGridSpec(
        num_scalar_prefetch=2,
        grid=(n // tm,),
        in_specs=[pl.BlockSpec((tm, d), lambda t, *_: (t, 0))],
        out_specs=pl.BlockSpec(memory_space=pl.ANY),
        scratch_shapes=[pltpu.SemaphoreType.DMA] * 2,
    )
    return pl.pallas_call(
        functools.partial(_dispatch_kernel, tm=tm),
        grid_spec=grid_spec,
        out_shape=jax.ShapeDtypeStruct((tile_fill.shape[0] * tm, d), x.dtype),
        compiler_params=_params(("arbitrary",)),
        name="dispatch_rows",
    )(pos, tile_fill, x)


def _routing_metadata(expert_idx, tm):
    n = expert_idx.shape[0]
    na = n * TOP_K
    n_tiles = na // tm + N_EXPERTS
    e_flat = expert_idx.reshape(na)
    onehot = (e_flat[:, None] == jnp.arange(N_EXPERTS, dtype=jnp.int32)[None, :]).astype(jnp.int32)
    csum = jnp.cumsum(onehot, axis=0)
    counts = csum[-1]
    rank = jnp.sum(onehot * csum, axis=1) - 1
    padded = ((counts + tm - 1) // tm) * tm
    ends = jnp.cumsum(padded)
    starts = ends - padded
    pos = (starts[e_flat] + rank).astype(jnp.int32)
    tile_start = jnp.arange(n_tiles, dtype=jnp.int32) * tm
    tile_expert = jnp.sum((tile_start[:, None] >= ends[None, :]).astype(jnp.int32), axis=1)
    tile_valid = (tile_start < ends[-1]).astype(jnp.int32)
    last_expert = jnp.max(jnp.where(counts > 0, jnp.arange(N_EXPERTS, dtype=jnp.int32), 0))
    tile_expert = jnp.where(tile_valid > 0, tile_expert, last_expert)
    is_last_of_expert = jnp.any((tile_start[:, None] + tm == ends[None, :]) & (counts > 0)[None, :], axis=1)
    tile_fill = (is_last_of_expert | (tile_valid == 0)).astype(jnp.int32)
    return pos, tile_fill, tile_expert, tile_valid


def _row_gather_start(src_hbm, row_of, base, dst, sem, n_rows, *, unrolled=False):
    def start(r):
        pltpu.make_async_copy(src_hbm.at[pl.ds(row_of(base + r), 1), :], dst.at[pl.ds(r, 1), :], sem).start()

    if unrolled:
        for r in range(n_rows):
            start(r)
    else:
        def body(r, carry):
            start(r)
            return carry
        lax.fori_loop(0, n_rows, body, 0, unroll=8)


def _row_gather_wait(src_hbm, dst, sem, n_rows):
    pltpu.make_async_copy(src_hbm.at[pl.ds(0, n_rows), :], dst, sem).wait()


def _grouped_swiglu_kernel(tile_expert_ref, tile_valid_ref, x_ref, wg_ref, wu_ref, wd_ref, o_ref, wgb, wub, wdb):
    t = pl.program_id(0)
    prev = jnp.maximum(t - 1, 0)
    valid = tile_valid_ref[t] > 0

    @pl.when(valid)
    def _():
        @pl.when((t == 0) | (tile_expert_ref[t] != tile_expert_ref[prev]))
        def _():
            wgb[...] = wg_ref[0].astype(BF16)
            wub[...] = wu_ref[0].astype(BF16)
            wdb[...] = wd_ref[0].astype(BF16)

        xb = x_ref[...].astype(BF16)
        hg = jnp.dot(xb, wgb[...], preferred_element_type=F32)
        hu = jnp.dot(xb, wub[...], preferred_element_type=F32)
        hcur = (hg / (1.0 + jnp.exp(-hg)) * hu).astype(BF16)
        o_ref[...] = jnp.dot(hcur, wdb[...], preferred_element_type=F32)

    @pl.when(jnp.logical_not(valid))
    def _():
        o_ref[...] = jnp.zeros_like(o_ref)


def grouped_swiglu(x_sorted, tile_expert, tile_valid, w_gate, w_up, w_down, *, tm=MOE_TM):
    d, f = w_gate.shape[-2:]
    n_tiles = tile_expert.shape[0]
    grid_spec = pltpu.PrefetchScalarGridSpec(
        num_scalar_prefetch=2,
        grid=(n_tiles,),
        in_specs=[pl.BlockSpec((tm, d), lambda t, te, tv: (t, 0)),
                  pl.BlockSpec((1, d, f), lambda t, te, tv: (te[t], 0, 0)),
                  pl.BlockSpec((1, d, f), lambda t, te, tv: (te[t], 0, 0)),
                  pl.BlockSpec((1, f, d), lambda t, te, tv: (te[t], 0, 0))],
        out_specs=pl.BlockSpec((tm, d), lambda t, te, tv: (t, 0)),
        scratch_shapes=[pltpu.VMEM((d, f), BF16), pltpu.VMEM((d, f), BF16), pltpu.VMEM((f, d), BF16)],
    )
    return pl.pallas_call(
        _grouped_swiglu_kernel,
        grid_spec=grid_spec,
        out_shape=jax.ShapeDtypeStruct((n_tiles * tm, d), F32),
        compiler_params=_params(("arbitrary",)),
        name="grouped_swiglu",
    )(tile_expert, tile_valid, x_sorted, w_gate, w_up, w_down)


def _combine_ln_kernel(pos_ref, rows_hbm, x_ref, wts_ref, g_ref, b_ref, *refs, tm):
    out_refs, (buf0, buf1, sem) = refs[:-3], refs[-3:]
    t = pl.program_id(0)
    slot = lax.rem(t, 2)

    def start(tile, sl, unrolled):
        _row_gather_start(rows_hbm, lambda a: pos_ref[2 * a], tile * tm, buf0.at[sl], sem.at[0, sl], tm,
                          unrolled=unrolled)
        _row_gather_start(rows_hbm, lambda a: pos_ref[2 * a + 1], tile * tm, buf1.at[sl], sem.at[1, sl], tm,
                          unrolled=unrolled)

    @pl.when(t == 0)
    def _():
        start(0, 0, False)

    _row_gather_wait(rows_hbm, buf0.at[slot], sem.at[0, slot], tm)
    _row_gather_wait(rows_hbm, buf1.at[slot], sem.at[1, slot], tm)

    @pl.when(t + 1 < pl.num_programs(0))
    def _():
        start(t + 1, 1 - slot, True)

    w = wts_ref[...]
    z = DEEPNORM_ALPHA * x_ref[...] + w[:, 0:1] * buf0[slot] + w[:, 1:2] * buf1[slot]
    y = _layer_norm_rows(z, g_ref[...], b_ref[...])
    for o_ref in out_refs:
        o_ref[...] = y.astype(o_ref.dtype)


def combine_residual_layernorm(rows, pos, x, wts, gain, bias, *, out_dtypes, tm=COMBINE_TM):
    n, d = x.shape
    tm = min(tm, n)
    grid_spec = pltpu.PrefetchScalarGridSpec(
        num_scalar_prefetch=1,
        grid=(n // tm,),
        in_specs=[pl.BlockSpec(memory_space=pl.ANY),
                  pl.BlockSpec((tm, d), lambda t, p: (t, 0)),
                  pl.BlockSpec((tm, TOP_K), lambda t, p: (t, 0)),
                  pl.BlockSpec((1, d), lambda t, p: (0, 0)),
                  pl.BlockSpec((1, d), lambda t, p: (0, 0))],
        out_specs=[pl.BlockSpec((tm, d), lambda t, p: (t, 0)) for _ in out_dtypes],
        scratch_shapes=[pltpu.VMEM((2, tm, d), F32), pltpu.VMEM((2, tm, d), F32),
                        pltpu.SemaphoreType.DMA((2, 2))],
    )
    return pl.pallas_call(
        functools.partial(_combine_ln_kernel, tm=tm),
        grid_spec=grid_spec,
        out_shape=[jax.ShapeDtypeStruct((n, d), dt) for dt in out_dtypes],
        compiler_params=_params(("arbitrary",)),
        name="moe_combine_ln",
    )(pos, rows, x, wts, gain.reshape(1, d), bias.reshape(1, d))


def moe_block(x, expert_idx, wts, w_gate, w_up, w_down, layer, gain, bias, *, out_dtypes, tm=MOE_TM):
    d, f = w_gate.shape[-2:]
    pos, tile_fill, tile_expert, tile_valid = _routing_metadata(expert_idx, tm)
    x_sorted = dispatch_rows(x, pos, tile_fill, tm)
    rows = grouped_swiglu(x_sorted, tile_expert + layer * N_EXPERTS, tile_valid,
                          w_gate.reshape(-1, d, f), w_up.reshape(-1, d, f), w_down.reshape(-1, f, d), tm=tm)
    return combine_residual_layernorm(rows, pos, x, wts, gain, bias, out_dtypes=out_dtypes)


def _rotary_tables(seq):
    half = RET_QK_DIM // 2
    freqs = ROPE_BASE ** (-jnp.arange(half, dtype=F32) / half)
    ang = jnp.arange(seq).astype(F32)[:, None] * freqs[None, :]
    return jnp.cos(ang), jnp.sin(ang)


def kernel(x, attn_w_in, attn_w_out, attn_rpb, ret_w_in, ret_w_out, ret_decay_fwd, ret_decay_bwd,
           moe_w_group_router, moe_b_group_router, moe_w_expert_router, moe_b_expert_router,
           moe_w_gate, moe_w_up, moe_w_down, ln_gain, ln_bias):
    b, s, d = x.shape
    n = b * s
    xf = x.reshape(n, d).astype(F32)
    xb = xf
    depth = ln_gain.shape[0]
    for i in range(depth):
        j = i // 2
        if i % 2 == 0:
            qkv = matmul(xb, attn_w_in[j].astype(F32))
            bias_tables = _attention_bias_tables(attn_rpb[j], s // GRID_W)
            mixed = neighbourhood_attention(qkv.reshape(b, s, 3 * d), bias_tables).reshape(n, d)
            w_out = attn_w_out[j]
        else:
            w_in = ret_w_in[j].astype(F32)
            qk_w, v_w = 2 * RET_HEADS * RET_QK_DIM, RET_HEADS * RET_V_DIM
            qk = matmul(xb, w_in, col_start=0, n_cols=qk_w, epilogue="rotary", rotary=_rotary_tables(s),
                        scale_from_col=qk_w // 2)
            v = matmul(xb, w_in, col_start=qk_w, n_cols=v_w)
            g = matmul(xb, w_in, col_start=qk_w + v_w, n_cols=v_w, epilogue="silu")
            lg = jnp.stack([jnp.log1p(-jnp.exp(ret_decay_fwd[j].astype(F32))),
                            jnp.log1p(-jnp.exp(ret_decay_bwd[j].astype(F32)))])
            mixed = retention(qk.reshape(b, s, qk_w), v.reshape(b, s, v_w), g.reshape(b, s, v_w),
                              lg).reshape(n, v_w)
            w_out = ret_w_out[j]
        whl, rbias = _router_operands(moe_w_group_router[i], moe_b_group_router[i], moe_w_expert_router[i],
                                      moe_b_expert_router[i])
        xf, expert_idx, wts = matmul_residual_layernorm_route(mixed, w_out.astype(BF16), xf, ln_gain[i, 0],
                                                              ln_bias[i, 0], whl, rbias)
        last = i == depth - 1
        outs = moe_block(xf, expert_idx, wts, moe_w_gate, moe_w_up, moe_w_down, i,
                         ln_gain[i, 1], ln_bias[i, 1], out_dtypes=(F32,) if last else (F32, BF16))
        xf, xb = outs[0], outs[-1]
    return xf.reshape(b, s, d).astype(x.dtype)
```

```python
import functools
import math

import jax
import jax.numpy as jnp
from jax import lax
from jax.experimental import pallas as pl
from jax.experimental.pallas import tpu as pltpu

GRID_W = 64
NA_HEADS = 16
NA_HEAD_DIM = 128
NA_MAX_KH = 8
NA_KW = 16
RET_HEADS = 8
RET_QK_DIM = 256
RET_V_DIM = 512
ROPE_BASE = 10000.0
N_GROUPS = 4
EXPERTS_PER_GROUP = 8
N_EXPERTS = N_GROUPS * EXPERTS_PER_GROUP
TOP_K = 2
DEPTH = 2
DEEPNORM_ALPHA = (2 * DEPTH) ** 0.25
LN_EPS = 1e-5
GN_EPS = 1e-6

V7X_LANES = 128
V7X_VMEM_LIMIT_BYTES = 56 * 1024 * 1024

BF16 = jnp.bfloat16
F32 = jnp.float32

MM_TM = 1024
MM_TN = 1024
LN_TM = 512
MOE_TM = 256
COMBINE_TM = 256
RET_CHUNK = 256
NA_ROW_GROUP = 4
NA_KEY_ROW_TILE = 4
ROUTER_LANES = V7X_LANES
METADATA_BLOCK = 256


def _params(semantics):
    return pltpu.CompilerParams(dimension_semantics=semantics, vmem_limit_bytes=V7X_VMEM_LIMIT_BYTES)


def _matmul_kernel(x_ref, w_ref, *refs, epilogue, scale_from_tile):
    o_ref, wb_ref = refs[-2:]

    @pl.when(pl.program_id(1) == 0)
    def _():
        wb_ref[...] = w_ref[...].astype(BF16)

    acc = jnp.dot(x_ref[...].astype(BF16), wb_ref[...], preferred_element_type=F32)
    if epilogue == "rotary":
        cos, sin = refs[0][...], refs[1][...]
        scale = jnp.where(pl.program_id(0) >= scale_from_tile, RET_QK_DIM ** -0.5, 1.0).astype(F32)
        half = RET_QK_DIM // 2
        for c0 in range(0, acc.shape[1], RET_QK_DIM):
            t1 = acc[:, c0:c0 + half]
            t2 = acc[:, c0 + half:c0 + RET_QK_DIM]
            o_ref[:, c0:c0 + half] = ((t1 * cos - t2 * sin) * scale).astype(o_ref.dtype)
            o_ref[:, c0 + half:c0 + RET_QK_DIM] = ((t1 * sin + t2 * cos) * scale).astype(o_ref.dtype)
    elif epilogue == "silu":
        o_ref[...] = (acc / (1.0 + jnp.exp(-acc))).astype(o_ref.dtype)
    else:
        o_ref[...] = acc.astype(o_ref.dtype)


def matmul(x, w, *, col_start=0, n_cols=None, epilogue=None, rotary=None, scale_from_col=None,
           tm=MM_TM, tn=MM_TN):
    m, k = x.shape
    n = w.shape[1] - col_start if n_cols is None else n_cols
    tm, tn = min(tm, m), min(tn, n)
    assert m % tm == 0 and n % tn == 0 and col_start % tn == 0
    tile0 = col_start // tn
    in_specs = [pl.BlockSpec((tm, k), lambda j, i: (i, 0)),
                pl.BlockSpec((k, tn), lambda j, i: (0, tile0 + j))]
    operands = [x, w]
    scale_from_tile = None
    if epilogue == "rotary":
        seq = rotary[0].shape[0]
        assert seq % tm == 0 and tn % RET_QK_DIM == 0 and scale_from_col % tn == 0
        in_specs += [pl.BlockSpec((tm, RET_QK_DIM // 2), lambda j, i: (i % (seq // tm), 0))] * 2
        operands += list(rotary)
        scale_from_tile = scale_from_col // tn
    return pl.pallas_call(
        functools.partial(_matmul_kernel, epilogue=epilogue, scale_from_tile=scale_from_tile),
        grid=(n // tn, m // tm),
        in_specs=in_specs,
        out_specs=pl.BlockSpec((tm, tn), lambda j, i: (i, j)),
        out_shape=jax.ShapeDtypeStruct((m, n), BF16),
        scratch_shapes=[pltpu.VMEM((k, tn), BF16)],
        compiler_params=_params(("parallel", "arbitrary")),
        name="proj_matmul",
    )(*operands)


def _layer_norm_rows(z, gain, bias):
    mu = jnp.mean(z, axis=-1, keepdims=True)
    zc = z - mu
    var = jnp.mean(zc * zc, axis=-1, keepdims=True)
    return zc * lax.rsqrt(var + LN_EPS) * gain + bias


LN_ROW_SPLIT = 2


def _mm_res_ln_kernel(a_ref, w_ref, x_ref, g_ref, b_ref, whl_ref, rb_ref, o_ref, idx_ref, wts_ref):
    hm = a_ref.shape[0] // LN_ROW_SPLIT
    for h in range(LN_ROW_SPLIT):
        rows = slice(h * hm, (h + 1) * hm)
        z = DEEPNORM_ALPHA * x_ref[rows, :] + jnp.dot(a_ref[rows, :], w_ref[...], preferred_element_type=F32)
        y = _layer_norm_rows(z, g_ref[...], b_ref[...])
        o_ref[rows, :] = y
        _route_rows(y, whl_ref, rb_ref, idx_ref, wts_ref, rows)


def matmul_residual_layernorm_route(a, w, x, gain, bias, router_whl, router_bias, *, tm=LN_TM):
    m, kdim = a.shape
    d = w.shape[1]
    tm = min(tm, m)
    assert m % tm == 0
    const = lambda i: (0, 0)
    row = lambda i: (i, 0)
    return pl.pallas_call(
        _mm_res_ln_kernel,
        grid=(m // tm,),
        in_specs=[pl.BlockSpec((tm, kdim), row),
                  pl.BlockSpec((kdim, d), const, pipeline_mode=pl.Buffered(1)),
                  pl.BlockSpec((tm, d), row),
                  pl.BlockSpec((1, d), const),
                  pl.BlockSpec((1, d), const),
                  pl.BlockSpec((d, 2 * ROUTER_LANES), const),
                  pl.BlockSpec((1, ROUTER_LANES), const)],
        out_specs=[pl.BlockSpec((tm, d), row), pl.BlockSpec((tm, TOP_K), row), pl.BlockSpec((tm, TOP_K), row)],
        out_shape=[jax.ShapeDtypeStruct((m, d), F32), jax.ShapeDtypeStruct((m, TOP_K), jnp.int32),
                   jax.ShapeDtypeStruct((m, TOP_K), F32)],
        compiler_params=_params(("parallel",)),
        name="outproj_residual_ln_route",
    )(a, w, x, gain.reshape(1, d), bias.reshape(1, d), router_whl, router_bias)


def _attention_groups(rows):
    kh = min(NA_MAX_KH, rows)
    g = min(NA_ROW_GROUP, rows)
    assert rows % g == 0
    row_start = lambda r: min(max(r - kh // 2, 0), rows - kh)
    groups, specs = [], []
    for r0 in range(0, rows, g):
        lo = row_start(r0)
        n = row_start(r0 + g - 1) + kh - lo
        n = min(-(-n // NA_KEY_ROW_TILE) * NA_KEY_ROW_TILE, rows)
        lo = min(lo, rows - n)
        spec = (n, tuple((r0 + j - lo, row_start(r0 + j) - lo) for j in range(g)))
        if spec not in specs:
            specs.append(spec)
        groups.append((r0, lo, n, specs.index(spec)))
    return g, groups, specs


def _attention_bias_tables(rpb, rows):
    kh = min(NA_MAX_KH, rows)
    cols = jnp.arange(GRID_W)
    col_start = jnp.clip(cols - NA_KW // 2, 0, GRID_W - NA_KW)
    col_in_win = (cols[None, :] >= col_start[:, None]) & (cols[None, :] < col_start[:, None] + NA_KW)
    dc = jnp.clip(cols[None, :] - cols[:, None], -(NA_KW - 1), NA_KW - 1) + NA_KW - 1
    rpb = rpb.astype(F32)
    nh, n_dr, n_dc = rpb.shape
    cexp = jnp.zeros((nh, GRID_W, n_dr, GRID_W), F32)
    for j in range(n_dc):
        cexp = jnp.where((dc == j)[None, :, None, :], rpb[:, None, :, j, None], cexp)
    cexp = jnp.where(col_in_win[None, :, None, :], cexp, -jnp.inf).reshape(nh, GRID_W, n_dr * GRID_W)
    tables = []
    for n, per_row in _attention_groups(rows)[2]:
        blocks = []
        for q_off, win_off in per_row:
            first = (win_off - q_off + NA_MAX_KH - 1) * GRID_W
            window = cexp[:, :, first:first + kh * GRID_W]
            blocks.append(jnp.pad(window, ((0, 0), (0, 0), (win_off * GRID_W, (n - win_off - kh) * GRID_W)),
                                  constant_values=-jnp.inf))
        tables.append(jnp.concatenate(blocks, axis=1))
    return tables


def _attention_kernel(q_ref, k_ref, v_ref, *refs, g, groups):
    bias_refs, o_ref = refs[:-1], refs[-1]
    scale = NA_HEAD_DIM ** -0.5
    for r0, lo, n, tid in groups:
        qs = slice(r0 * GRID_W, (r0 + g) * GRID_W)
        ks = slice(lo * GRID_W, (lo + n) * GRID_W)
        s = lax.dot_general(q_ref[0, qs, :], k_ref[0, ks, :], (((1,), (1,)), ((), ())),
                            preferred_element_type=F32)
        s = s * scale + bias_refs[tid][0]
        m = jnp.max(s, axis=-1, keepdims=True)
        p = jnp.exp(s - m)
        l = jnp.sum(p, axis=-1, keepdims=True)
        o = jnp.dot(p.astype(BF16), v_ref[0, ks, :], preferred_element_type=F32) / l
        o_ref[0, qs, :] = o.astype(o_ref.dtype)


def neighbourhood_attention(qkv, bias_tables):
    b, s, d3 = qkv.shape
    d = d3 // 3
    nh = NA_HEADS
    hd = NA_HEAD_DIM
    g, groups, _ = _attention_groups(s // GRID_W)
    bias_specs = [pl.BlockSpec((1,) + t.shape[1:], lambda h, i: (h, 0, 0)) for t in bias_tables]
    return pl.pallas_call(
        functools.partial(_attention_kernel, g=g, groups=groups),
        grid=(nh, b),
        in_specs=[pl.BlockSpec((1, s, hd), lambda h, i: (i, 0, h)),
                  pl.BlockSpec((1, s, hd), lambda h, i: (i, 0, nh + h)),
                  pl.BlockSpec((1, s, hd), lambda h, i: (i, 0, 2 * nh + h))] + bias_specs,
        out_specs=pl.BlockSpec((1, s, hd), lambda h, i: (i, 0, h)),
        out_shape=jax.ShapeDtypeStruct((b, s, d), BF16),
        compiler_params=_params(("parallel", "arbitrary")),
        name="neighbourhood_attention",
    )(qkv, qkv, qkv, *bias_tables)


def _retention_kernel(lg_ref, q_ref, k_ref, v_ref, g_ref, o_ref, tb_ref, *, seq, chunk):
    h = pl.program_id(1)
    lg_f = lg_ref[0, h]
    lg_b = lg_ref[1, h]
    n_chunks = seq // chunk

    ia = lax.broadcasted_iota(jnp.int32, (chunk, chunk), 0)
    ic = lax.broadcasted_iota(jnp.int32, (chunk, chunk), 1)
    diff = (ia - ic).astype(F32)
    dmat = jnp.where(diff >= 0, jnp.exp(lg_f * jnp.maximum(diff, 0.0)), jnp.exp(lg_b * jnp.maximum(-diff, 0.0)))
    idx = lax.broadcasted_iota(jnp.int32, (chunk, 1), 0).astype(F32)
    qdec_f = jnp.exp(lg_f * (idx + 1.0))
    kdec_f = jnp.exp(lg_f * (chunk - 1.0 - idx))
    qdec_b = jnp.exp(lg_b * (chunk - idx))
    kdec_b = jnp.exp(lg_b * idx)
    cdec_f = jnp.exp(lg_f * chunk)
    cdec_b = jnp.exp(lg_b * chunk)

    def kv_outer(i, kdec):
        ks = (k_ref[0, i * chunk:(i + 1) * chunk, :].astype(F32) * kdec).astype(BF16)
        vi = v_ref[0, i * chunk:(i + 1) * chunk, :]
        return lax.dot_general(ks, vi, (((0,), (0,)), ((), ())), preferred_element_type=F32)

    state = jnp.zeros((RET_QK_DIM, RET_V_DIM), F32)
    for i in range(n_chunks - 1, -1, -1):
        tb_ref[i] = state.astype(BF16)
        if i > 0:
            state = state * cdec_b + kv_outer(i, kdec_b)

    state = jnp.zeros((RET_QK_DIM, RET_V_DIM), F32)
    for i in range(n_chunks):
        sl = slice(i * chunk, (i + 1) * chunk)
        qi = q_ref[0, sl, :]
        ki = k_ref[0, sl, :]
        vi = v_ref[0, sl, :]
        s = lax.dot_general(qi, ki, (((1,), (1,)), ((), ())), preferred_element_type=F32)
        y = jnp.dot((s * dmat).astype(BF16), vi, preferred_element_type=F32)
        y = y + jnp.dot(qi, tb_ref[i], preferred_element_type=F32) * qdec_b
        if i > 0:
            y = y + jnp.dot(qi, state.astype(BF16), preferred_element_type=F32) * qdec_f
        if i < n_chunks - 1:
            state = state * cdec_f + kv_outer(i, kdec_f)
        y = y * lax.rsqrt(jnp.mean(y * y, axis=-1, keepdims=True) + GN_EPS)
        o_ref[0, sl, :] = (g_ref[0, sl, :].astype(F32) * y).astype(o_ref.dtype)


def retention(qk, v, g, lg):
    b, s, _ = qk.shape
    nh = RET_HEADS
    dk, dv = RET_QK_DIM, RET_V_DIM
    chunk = min(RET_CHUNK, s)
    assert s % chunk == 0
    return pl.pallas_call(
        functools.partial(_retention_kernel, seq=s, chunk=chunk),
        grid=(b, nh),
        in_specs=[pl.BlockSpec(memory_space=pltpu.SMEM),
                  pl.BlockSpec((1, s, dk), lambda i, h: (i, 0, h)),
                  pl.BlockSpec((1, s, dk), lambda i, h: (i, 0, nh + h)),
                  pl.BlockSpec((1, s, dv), lambda i, h: (i, 0, h)),
                  pl.BlockSpec((1, s, dv), lambda i, h: (i, 0, h))],
        out_specs=pl.BlockSpec((1, s, dv), lambda i, h: (i, 0, h)),
        out_shape=jax.ShapeDtypeStruct((b, s, nh * dv), BF16),
        scratch_shapes=[pltpu.VMEM((s // chunk, dk, dv), BF16)],
        compiler_params=_params(("parallel", "arbitrary")),
        name="retention",
    )(lg, qk, qk, v, g)


def _route_rows(x, whl_ref, b_ref, idx_ref, wts_ref, rows):
    xh = x.astype(BF16)
    xl = (x - xh.astype(F32)).astype(BF16)
    both = jnp.dot(xh, whl_ref[...], preferred_element_type=F32)
    logits = (both[:, :ROUTER_LANES] + both[:, ROUTER_LANES:]
              + jnp.dot(xl, whl_ref[:, :ROUTER_LANES], preferred_element_type=F32)) + b_ref[...]
    lane = lax.broadcasted_iota(jnp.int32, logits.shape, 1)
    neg = -jnp.inf
    big = jnp.int32(ROUTER_LANES)

    gl = jnp.where(lane < N_GROUPS, logits, neg)
    gmax = jnp.max(gl, axis=-1, keepdims=True)
    g_sel = jnp.min(jnp.where(gl == gmax, lane, big), axis=-1, keepdims=True)
    w_grp = 1.0 / jnp.sum(jnp.exp(gl - gmax), axis=-1, keepdims=True)

    lo = N_GROUPS + EXPERTS_PER_GROUP * g_sel
    el = jnp.where((lane >= lo) & (lane < lo + EXPERTS_PER_GROUP), logits, neg)
    v1 = jnp.max(el, axis=-1, keepdims=True)
    i1 = jnp.min(jnp.where(el == v1, lane, big), axis=-1, keepdims=True)
    el2 = jnp.where(lane == i1, neg, el)
    v2 = jnp.max(el2, axis=-1, keepdims=True)
    i2 = jnp.min(jnp.where(el2 == v2, lane, big), axis=-1, keepdims=True)
    e21 = jnp.exp(v2 - v1)
    p1 = 1.0 / (1.0 + e21)
    p2 = e21 * p1
    idx_ref[rows, 0:1] = i1 - N_GROUPS
    idx_ref[rows, 1:2] = i2 - N_GROUPS
    wts_ref[rows, 0:1] = p1 * w_grp
    wts_ref[rows, 1:2] = p2 * w_grp


def _router_operands(w_group, b_group, w_expert, b_expert):
    d = w_group.shape[0]
    w = jnp.concatenate([w_group.astype(F32), w_expert.astype(F32).reshape(d, N_EXPERTS)], axis=1)
    w = jnp.pad(w, ((0, 0), (0, ROUTER_LANES - w.shape[1])))
    wh = w.astype(BF16)
    wl = (w - wh.astype(F32)).astype(BF16)
    bias = jnp.concatenate([b_group.astype(F32), b_expert.astype(F32).reshape(N_EXPERTS)])
    bias = jnp.pad(bias, (0, ROUTER_LANES - bias.shape[0])).reshape(1, ROUTER_LANES)
    return jnp.concatenate([wh, wl], axis=1), bias


def _dispatch_kernel(pos_ref, tile_fill_ref, x_ref, xs_hbm, sem, fill_sem, *, tm):
    t = pl.program_id(0)

    @pl.when(t == 0)
    def _():
        def tile_body(j, n_started):
            @pl.when(tile_fill_ref[j] > 0)
            def _():
                dst = xs_hbm.at[pl.ds(pl.multiple_of(j * tm, tm), tm), :]
                pltpu.make_async_copy(x_ref, dst, fill_sem).start()
            return n_started + tile_fill_ref[j]
        n_fills = lax.fori_loop(0, tile_fill_ref.shape[0], tile_body, 0)

        def wait_fill(i, carry):
            pltpu.make_async_copy(x_ref, xs_hbm.at[pl.ds(0, tm), :], fill_sem).wait()
            return carry
        lax.fori_loop(0, n_fills, wait_fill, 0)

    for i in range(tm):
        for k in range(TOP_K):
            dst = xs_hbm.at[pl.ds(pos_ref[(t * tm + i) * TOP_K + k], 1), :]
            pltpu.make_async_copy(x_ref.at[pl.ds(i, 1), :], dst, sem).start()
    for _ in range(TOP_K):
        pltpu.make_async_copy(x_ref, xs_hbm.at[pl.ds(0, tm), :], sem).wait()


def dispatch_rows(x, pos, tile_fill, tm):
    n, d = x.shape
    assert n % tm == 0
    grid_spec = pltpu.PrefetchScalarGridSpec(
        num_scalar_prefetch=2,
        grid=(n // tm,),
        in_specs=[pl.BlockSpec((tm, d), lambda t, *_: (t, 0))],
        out_specs=pl.BlockSpec(memory_space=pl.ANY),
        scratch_shapes=[pltpu.SemaphoreType.DMA] * 2,
    )
    return pl.pallas_call(
        functools.partial(_dispatch_kernel, tm=tm),
        grid_spec=grid_spec,
        out_shape=jax.ShapeDtypeStruct((tile_fill.shape[0] * tm, d), x.dtype),
        compiler_params=_params(("arbitrary",)),
        name="dispatch_rows",
    )(pos, tile_fill, x)


def _routing_metadata(expert_idx, tm):
    n = expert_idx.shape[0]
    na = n * TOP_K
    n_tiles = na // tm + N_EXPERTS
    e_flat = expert_idx.reshape(na)
    onehot = (e_flat[:, None] == jnp.arange(N_EXPERTS, dtype=jnp.int32)[None, :]).astype(F32)
    blk = min(METADATA_BLOCK, na)
    oh_blocks = onehot.reshape(na // blk, blk, N_EXPERTS)
    within = jnp.einsum("ij,bje->bie", jnp.tril(jnp.ones((blk, blk), F32)), oh_blocks,
                        precision=lax.Precision.HIGHEST)
    block_total = within[:, -1, :]
    block_offset = jnp.cumsum(block_total, axis=0) - block_total
    csum = (within + block_offset[:, None, :]).reshape(na, N_EXPERTS)
    counts = (block_offset[-1] + block_total[-1]).astype(jnp.int32)
    rank = jnp.sum(onehot * csum, axis=1).astype(jnp.int32) - 1
    padded = ((counts + tm - 1) // tm) * tm
    ends = jnp.cumsum(padded)
    starts = ends - padded
    pos = (starts[e_flat] + rank).astype(jnp.int32)
    tile_start = jnp.arange(n_tiles, dtype=jnp.int32) * tm
    tile_expert = jnp.sum((tile_start[:, None] >= ends[None, :]).astype(jnp.int32), axis=1)
    tile_valid = (tile_start < ends[-1]).astype(jnp.int32)
    last_expert = jnp.max(jnp.where(counts > 0, jnp.arange(N_EXPERTS, dtype=jnp.int32), 0))
    tile_expert = jnp.where(tile_valid > 0, tile_expert, last_expert)
    is_last_of_expert = jnp.any((tile_start[:, None] + tm == ends[None, :]) & (counts > 0)[None, :], axis=1)
    tile_fill = (is_last_of_expert | (tile_valid == 0)).astype(jnp.int32)
    return pos, tile_fill, tile_expert, tile_valid


def _row_gather_start(src_hbm, row_of, base, dst, sem, n_rows, *, unrolled=False):
    def start(r):
        pltpu.make_async_copy(src_hbm.at[pl.ds(row_of(base + r), 1), :], dst.at[pl.ds(r, 1), :], sem).start()

    if unrolled:
        for r in range(n_rows):
            start(r)
    else:
        def body(r, carry):
            start(r)
            return carry
        lax.fori_loop(0, n_rows, body, 0, unroll=8)


def _row_gather_wait(src_hbm, dst, sem, n_rows):
    pltpu.make_async_copy(src_hbm.at[pl.ds(0, n_rows), :], dst, sem).wait()


def _grouped_swiglu_kernel(tile_expert_ref, tile_valid_ref, x_ref, wg_ref, wu_ref, wd_ref, o_ref, wgb, wub, wdb):
    t = pl.program_id(0)
    prev = jnp.maximum(t - 1, 0)
    valid = tile_valid_ref[t] > 0

    @pl.when(valid)
    def _():
        @pl.when((t == 0) | (tile_expert_ref[t] != tile_expert_ref[prev]))
        def _():
            wgb[...] = wg_ref[0].astype(BF16)
            wub[...] = wu_ref[0].astype(BF16)
            wdb[...] = wd_ref[0].astype(BF16)

        xb = x_ref[...].astype(BF16)
        hg = jnp.dot(xb, wgb[...], preferred_element_type=F32)
        hu = jnp.dot(xb, wub[...], preferred_element_type=F32)
        hcur = (hg / (1.0 + jnp.exp(-hg)) * hu).astype(BF16)
        o_ref[...] = jnp.dot(hcur, wdb[...], preferred_element_type=F32)

    @pl.when(jnp.logical_not(valid))
    def _():
        o_ref[...] = jnp.zeros_like(o_ref)


def grouped_swiglu(x_sorted, tile_expert, tile_valid, w_gate, w_up, w_down, *, tm=MOE_TM):
    d, f = w_gate.shape[-2:]
    n_tiles = tile_expert.shape[0]
    grid_spec = pltpu.PrefetchScalarGridSpec(
        num_scalar_prefetch=2,
        grid=(n_tiles,),
        in_specs=[pl.BlockSpec((tm, d), lambda t, te, tv: (t, 0)),
                  pl.BlockSpec((1, d, f), lambda t, te, tv: (te[t], 0, 0)),
                  pl.BlockSpec((1, d, f), lambda t, te, tv: (te[t], 0, 0)),
                  pl.BlockSpec((1, f, d), lambda t, te, tv: (te[t], 0, 0))],
        out_specs=pl.BlockSpec((tm, d), lambda t, te, tv: (t, 0)),
        scratch_shapes=[pltpu.VMEM((d, f), BF16), pltpu.VMEM((d, f), BF16), pltpu.VMEM((f, d), BF16)],
    )
    return pl.pallas_call(
        _grouped_swiglu_kernel,
        grid_spec=grid_spec,
        out_shape=jax.ShapeDtypeStruct((n_tiles * tm, d), F32),
        compiler_params=_params(("arbitrary",)),
        name="grouped_swiglu",
    )(tile_expert, tile_valid, x_sorted, w_gate, w_up, w_down)


def _combine_ln_kernel(pos_ref, rows_hbm, x_ref, wts_ref, g_ref, b_ref, *refs, tm):
    out_refs, (buf0, buf1, sem) = refs[:-3], refs[-3:]
    t = pl.program_id(0)
    slot = lax.rem(t, 2)

    def start(tile, sl, unrolled):
        _row_gather_start(rows_hbm, lambda a: pos_ref[2 * a], tile * tm, buf0.at[sl], sem.at[0, sl], tm,
                          unrolled=unrolled)
        _row_gather_start(rows_hbm, lambda a: pos_ref[2 * a + 1], tile * tm, buf1.at[sl], sem.at[1, sl], tm,
                          unrolled=unrolled)

    @pl.when(t == 0)
    def _():
        start(0, 0, False)

    _row_gather_wait(rows_hbm, buf0.at[slot], sem.at[0, slot], tm)
    _row_gather_wait(rows_hbm, buf1.at[slot], sem.at[1, slot], tm)

    @pl.when(t + 1 < pl.num_programs(0))
    def _():
        start(t + 1, 1 - slot, True)

    w = wts_ref[...]
    z = DEEPNORM_ALPHA * x_ref[...] + w[:, 0:1] * buf0[slot] + w[:, 1:2] * buf1[slot]
    y = _layer_norm_rows(z, g_ref[...], b_ref[...])
    for o_ref in out_refs:
        o_ref[...] = y.astype(o_ref.dtype)


def combine_residual_layernorm(rows, pos, x, wts, gain, bias, *, out_dtypes, tm=COMBINE_TM):
    n, d = x.shape
    tm = min(tm, n)
    grid_spec = pltpu.PrefetchScalarGridSpec(
        num_scalar_prefetch=1,
        grid=(n // tm,),
        in_specs=[pl.BlockSpec(memory_space=pl.ANY),
                  pl.BlockSpec((tm, d), lambda t, p: (t, 0)),
                  pl.BlockSpec((tm, TOP_K), lambda t, p: (t, 0)),
                  pl.BlockSpec((1, d), lambda t, p: (0, 0)),
                  pl.BlockSpec((1, d), lambda t, p: (0, 0))],
        out_specs=[pl.BlockSpec((tm, d), lambda t, p: (t, 0)) for _ in out_dtypes],
        scratch_shapes=[pltpu.VMEM((2, tm, d), F32), pltpu.VMEM((2, tm, d), F32),
                        pltpu.SemaphoreType.DMA((2, 2))],
    )
    return pl.pallas_call(
        functools.partial(_combine_ln_kernel, tm=tm),
        grid_spec=grid_spec,
        out_shape=[jax.ShapeDtypeStruct((n, d), dt) for dt in out_dtypes],
        compiler_params=_params(("arbitrary",)),
        name="moe_combine_ln",
    )(pos, rows, x, wts, gain.reshape(1, d), bias.reshape(1, d))


def moe_block(x, expert_idx, wts, w_gate, w_up, w_down, layer, gain, bias, *, out_dtypes, tm=MOE_TM):
    d, f = w_gate.shape[-2:]
    pos, tile_fill, tile_expert, tile_valid = _routing_metadata(expert_idx, tm)
    x_sorted = dispatch_rows(x, pos, tile_fill, tm)
    rows = grouped_swiglu(x_sorted, tile_expert + layer * N_EXPERTS, tile_valid,
                          w_gate.reshape(-1, d, f), w_up.reshape(-1, d, f), w_down.reshape(-1, f, d), tm=tm)
    return combine_residual_layernorm(rows, pos, x, wts, gain, bias, out_dtypes=out_dtypes)


def _rotary_tables(seq):
    half = RET_QK_DIM // 2
    freqs = ROPE_BASE ** (-jnp.arange(half, dtype=F32) / half)
    ang = jnp.arange(seq).astype(F32)[:, None] * freqs[None, :]
    return jnp.cos(ang), jnp.sin(ang)


def kernel(x, attn_w_in, attn_w_out, attn_rpb, ret_w_in, ret_w_out, ret_decay_fwd, ret_decay_bwd,
           moe_w_group_router, moe_b_group_router, moe_w_expert_router, moe_b_expert_router,
           moe_w_gate, moe_w_up, moe_w_down, ln_gain, ln_bias):
    b, s, d = x.shape
    n = b * s
    xf = x.reshape(n, d).astype(F32)
    xb = xf
    depth = ln_gain.shape[0]
    for i in range(depth):
        j = i // 2
        if i % 2 == 0:
            qkv = matmul(xb, attn_w_in[j].astype(F32))
            bias_tables = _attention_bias_tables(attn_rpb[j], s // GRID_W)
            mixed = neighbourhood_attention(qkv.reshape(b, s, 3 * d), bias_tables).reshape(n, d)
            w_out = attn_w_out[j]
        else:
            w_in = ret_w_in[j].astype(F32)
            qk_w, v_w = 2 * RET_HEADS * RET_QK_DIM, RET_HEADS * RET_V_DIM
            qk = matmul(xb, w_in, col_start=0, n_cols=qk_w, epilogue="rotary", rotary=_rotary_tables(s),
                        scale_from_col=qk_w // 2)
            v = matmul(xb, w_in, col_start=qk_w, n_cols=v_w)
            g = matmul(xb, w_in, col_start=qk_w + v_w, n_cols=v_w, epilogue="silu")
            lg = jnp.stack([jnp.log1p(-jnp.exp(ret_decay_fwd[j].astype(F32))),
                            jnp.log1p(-jnp.exp(ret_decay_bwd[j].astype(F32)))])
            mixed = retention(qk.reshape(b, s, qk_w), v.reshape(b, s, v_w), g.reshape(b, s, v_w),
                              lg).reshape(n, v_w)
            w_out = ret_w_out[j]
        whl, rbias = _router_operands(moe_w_group_router[i], moe_b_group_router[i], moe_w_expert_router[i],
                                      moe_b_expert_router[i])
        xf, expert_idx, wts = matmul_residual_layernorm_route(mixed, w_out.astype(BF16), xf, ln_gain[i, 0],
                                                              ln_bias[i, 0], whl, rbias)
        last = i == depth - 1
        outs = moe_block(xf, expert_idx, wts, moe_w_gate, moe_w_up, moe_w_down, i,
                         ln_gain[i, 1], ln_bias[i, 1], out_dtypes=(F32,) if last else (F32, BF16))
        xf, xb = outs[0], outs[-1]
    return xf.reshape(b, s, d).astype(x.dtype)
```

```python
import functools
import math

import jax
import jax.numpy as jnp
from jax import lax
from jax.experimental import pallas as pl
from jax.experimental.pallas import tpu as pltpu

GRID_W = 64
NA_HEADS = 16
NA_HEAD_DIM = 128
NA_MAX_KH = 8
NA_KW = 16
RET_HEADS = 8
RET_QK_DIM = 256
RET_V_DIM = 512
ROPE_BASE = 10000.0
N_GROUPS = 4
EXPERTS_PER_GROUP = 8
N_EXPERTS = N_GROUPS * EXPERTS_PER_GROUP
TOP_K = 2
DEPTH = 2
DEEPNORM_ALPHA = (2 * DEPTH) ** 0.25
LN_EPS = 1e-5
GN_EPS = 1e-6

V7X_LANES = 128
V7X_VMEM_LIMIT_BYTES = 56 * 1024 * 1024

BF16 = jnp.bfloat16
F32 = jnp.float32

MM_TM = 1024
MM_TN = 1024
LN_TM = 512
MOE_TM = 256
COMBINE_TM = 256
RET_CHUNK = 256
NA_ROW_GROUP = 4
NA_KEY_ROW_TILE = 4
ROUTER_LANES = V7X_LANES
METADATA_BLOCK = 256


def _params(semantics):
    return pltpu.CompilerParams(dimension_semantics=semantics, vmem_limit_bytes=V7X_VMEM_LIMIT_BYTES)


def _matmul_kernel(x_ref, w_ref, *refs, epilogue, scale_from_tile):
    o_ref, wb_ref = refs[-2:]

    @pl.when(pl.program_id(1) == 0)
    def _():
        wb_ref[...] = w_ref[...].astype(BF16)

    acc = jnp.dot(x_ref[...].astype(BF16), wb_ref[...], preferred_element_type=F32)
    if epilogue == "rotary":
        cos, sin = refs[0][...], refs[1][...]
        scale = jnp.where(pl.program_id(0) >= scale_from_tile, RET_QK_DIM ** -0.5, 1.0).astype(F32)
        half = RET_QK_DIM // 2
        for c0 in range(0, acc.shape[1], RET_QK_DIM):
            t1 = acc[:, c0:c0 + half]
            t2 = acc[:, c0 + half:c0 + RET_QK_DIM]
            o_ref[:, c0:c0 + half] = ((t1 * cos - t2 * sin) * scale).astype(o_ref.dtype)
            o_ref[:, c0 + half:c0 + RET_QK_DIM] = ((t1 * sin + t2 * cos) * scale).astype(o_ref.dtype)
    elif epilogue == "silu":
        o_ref[...] = (acc / (1.0 + jnp.exp(-acc))).astype(o_ref.dtype)
    else:
        o_ref[...] = acc.astype(o_ref.dtype)


def matmul(x, w, *, col_start=0, n_cols=None, epilogue=None, rotary=None, scale_from_col=None,
           tm=MM_TM, tn=MM_TN):
    m, k = x.shape
    n = w.shape[1] - col_start if n_cols is None else n_cols
    tm, tn = min(tm, m), min(tn, n)
    assert m % tm == 0 and n % tn == 0 and col_start % tn == 0
    tile0 = col_start // tn
    in_specs = [pl.BlockSpec((tm, k), lambda j, i: (i, 0)),
                pl.BlockSpec((k, tn), lambda j, i: (0, tile0 + j))]
    operands = [x, w]
    scale_from_tile = None
    if epilogue == "rotary":
        seq = rotary[0].shape[0]
        assert seq % tm == 0 and tn % RET_QK_DIM == 0 and scale_from_col % tn == 0
        in_specs += [pl.BlockSpec((tm, RET_QK_DIM // 2), lambda j, i: (i % (seq // tm), 0))] * 2
        operands += list(rotary)
        scale_from_tile = scale_from_col // tn
    return pl.pallas_call(
        functools.partial(_matmul_kernel, epilogue=epilogue, scale_from_tile=scale_from_tile),
        grid=(n // tn, m // tm),
        in_specs=in_specs,
        out_specs=pl.BlockSpec((tm, tn), lambda j, i: (i, j)),
        out_shape=jax.ShapeDtypeStruct((m, n), BF16),
        scratch_shapes=[pltpu.VMEM((k, tn), BF16)],
        compiler_params=_params(("parallel", "arbitrary")),
        name="proj_matmul",
    )(*operands)


def _layer_norm_rows(z, gain, bias):
    mu = jnp.mean(z, axis=-1, keepdims=True)
    zc = z - mu
    var = jnp.mean(zc * zc, axis=-1, keepdims=True)
    return zc * lax.rsqrt(var + LN_EPS) * gain + bias


LN_ROW_SPLIT = 2


def _mm_res_ln_kernel(a_ref, w_ref, x_ref, g_ref, b_ref, whl_ref, rb_ref, o_ref, idx_ref, wts_ref):
    hm = a_ref.shape[0] // LN_ROW_SPLIT
    for h in range(LN_ROW_SPLIT):
        rows = slice(h * hm, (h + 1) * hm)
        z = DEEPNORM_ALPHA * x_ref[rows, :] + jnp.dot(a_ref[rows, :], w_ref[...], preferred_element_type=F32)
        y = _layer_norm_rows(z, g_ref[...], b_ref[...])
        o_ref[rows, :] = y
        _route_rows(y, whl_ref, rb_ref, idx_ref, wts_ref, rows)


def matmul_residual_layernorm_route(a, w, x, gain, bias, router_whl, router_bias, *, tm=LN_TM):
    m, kdim = a.shape
    d = w.shape[1]
    tm = min(tm, m)
    assert m % tm == 0
    const = lambda i: (0, 0)
    row = lambda i: (i, 0)
    return pl.pallas_call(
        _mm_res_ln_kernel,
        grid=(m // tm,),
        in_specs=[pl.BlockSpec((tm, kdim), row),
                  pl.BlockSpec((kdim, d), const, pipeline_mode=pl.Buffered(1)),
                  pl.BlockSpec((tm, d), row),
                  pl.BlockSpec((1, d), const),
                  pl.BlockSpec((1, d), const),
                  pl.BlockSpec((d, 2 * ROUTER_LANES), const),
                  pl.BlockSpec((1, ROUTER_LANES), const)],
        out_specs=[pl.BlockSpec((tm, d), row), pl.BlockSpec((tm, TOP_K), row), pl.BlockSpec((tm, TOP_K), row)],
        out_shape=[jax.ShapeDtypeStruct((m, d), F32), jax.ShapeDtypeStruct((m, TOP_K), jnp.int32),
                   jax.ShapeDtypeStruct((m, TOP_K), F32)],
        compiler_params=_params(("parallel",)),
        name="outproj_residual_ln_route",
    )(a, w, x, gain.reshape(1, d), bias.reshape(1, d), router_whl, router_bias)


def _attention_groups(rows):
    kh = min(NA_MAX_KH, rows)
    g = min(NA_ROW_GROUP, rows)
    assert rows % g == 0
    row_start = lambda r: min(max(r - kh // 2, 0), rows - kh)
    groups, specs = [], []
    for r0 in range(0, rows, g):
        lo = row_start(r0)
        n = row_start(r0 + g - 1) + kh - lo
        n = min(-(-n // NA_KEY_ROW_TILE) * NA_KEY_ROW_TILE, rows)
        lo = min(lo, rows - n)
        spec = (n, tuple((r0 + j - lo, row_start(r0 + j) - lo) for j in range(g)))
        if spec not in specs:
            specs.append(spec)
        groups.append((r0, lo, n, specs.index(spec)))
    return g, groups, specs


def _attention_bias_tables(rpb, rows):
    kh = min(NA_MAX_KH, rows)
    cols = jnp.arange(GRID_W)
    col_start = jnp.clip(cols - NA_KW // 2, 0, GRID_W - NA_KW)
    col_in_win = (cols[None, :] >= col_start[:, None]) & (cols[None, :] < col_start[:, None] + NA_KW)
    dc = jnp.clip(cols[None, :] - cols[:, None], -(NA_KW - 1), NA_KW - 1) + NA_KW - 1
    rpb = rpb.astype(F32)
    nh, n_dr, n_dc = rpb.shape
    cexp = jnp.zeros((nh, GRID_W, n_dr, GRID_W), F32)
    for j in range(n_dc):
        cexp = jnp.where((dc == j)[None, :, None, :], rpb[:, None, :, j, None], cexp)
    cexp = jnp.where(col_in_win[None, :, None, :], cexp, -jnp.inf).reshape(nh, GRID_W, n_dr * GRID_W)
    tables = []
    for n, per_row in _attention_groups(rows)[2]:
        blocks = []
        for q_off, win_off in per_row:
            first = (win_off - q_off + NA_MAX_KH - 1) * GRID_W
            window = cexp[:, :, first:first + kh * GRID_W]
            blocks.append(jnp.pad(window, ((0, 0), (0, 0), (win_off * GRID_W, (n - win_off - kh) * GRID_W)),
                                  constant_values=-jnp.inf))
        tables.append(jnp.concatenate(blocks, axis=1))
    return tables


def _attention_kernel(q_ref, k_ref, v_ref, *refs, g, groups):
    bias_refs, o_ref = refs[:-1], refs[-1]
    scale = NA_HEAD_DIM ** -0.5
    for r0, lo, n, tid in groups:
        qs = slice(r0 * GRID_W, (r0 + g) * GRID_W)
        ks = slice(lo * GRID_W, (lo + n) * GRID_W)
        s = lax.dot_general(q_ref[0, qs, :], k_ref[0, ks, :], (((1,), (1,)), ((), ())),
                            preferred_element_type=F32)
        s = s * scale + bias_refs[tid][0]
        m = jnp.max(s, axis=-1, keepdims=True)
        p = jnp.exp(s - m).astype(BF16)
        v_ones = jnp.concatenate([v_ref[0, ks, :], jnp.ones((n * GRID_W, NA_HEAD_DIM), BF16)], axis=1)
        ol = jnp.dot(p, v_ones, preferred_element_type=F32)
        o = ol[:, :NA_HEAD_DIM] / ol[:, NA_HEAD_DIM:NA_HEAD_DIM + 1]
        o_ref[0, qs, :] = o.astype(o_ref.dtype)


def neighbourhood_attention(qkv, bias_tables):
    b, s, d3 = qkv.shape
    d = d3 // 3
    nh = NA_HEADS
    hd = NA_HEAD_DIM
    g, groups, _ = _attention_groups(s // GRID_W)
    bias_specs = [pl.BlockSpec((1,) + t.shape[1:], lambda h, i: (h, 0, 0)) for t in bias_tables]
    return pl.pallas_call(
        functools.partial(_attention_kernel, g=g, groups=groups),
        grid=(nh, b),
        in_specs=[pl.BlockSpec((1, s, hd), lambda h, i: (i, 0, h)),
                  pl.BlockSpec((1, s, hd), lambda h, i: (i, 0, nh + h)),
                  pl.BlockSpec((1, s, hd), lambda h, i: (i, 0, 2 * nh + h))] + bias_specs,
        out_specs=pl.BlockSpec((1, s, hd), lambda h, i: (i, 0, h)),
        out_shape=jax.ShapeDtypeStruct((b, s, d), BF16),
        compiler_params=_params(("parallel", "arbitrary")),
        name="neighbourhood_attention",
    )(qkv, qkv, qkv, *bias_tables)


def _retention_kernel(lg_ref, q_ref, k_ref, v_ref, g_ref, o_ref, tb_ref, *, seq, chunk):
    h = pl.program_id(1)
    lg_f = lg_ref[0, h]
    lg_b = lg_ref[1, h]
    n_chunks = seq // chunk

    ia = lax.broadcasted_iota(jnp.int32, (chunk, chunk), 0)
    ic = lax.broadcasted_iota(jnp.int32, (chunk, chunk), 1)
    diff = (ia - ic).astype(F32)
    dmat = jnp.where(diff >= 0, jnp.exp(lg_f * jnp.maximum(diff, 0.0)), jnp.exp(lg_b * jnp.maximum(-diff, 0.0)))
    idx = lax.broadcasted_iota(jnp.int32, (chunk, 1), 0).astype(F32)
    qdec_f = jnp.exp(lg_f * (idx + 1.0))
    kdec_f = jnp.exp(lg_f * (chunk - 1.0 - idx))
    qdec_b = jnp.exp(lg_b * (chunk - idx))
    kdec_b = jnp.exp(lg_b * idx)
    cdec_f = jnp.exp(lg_f * chunk)
    cdec_b = jnp.exp(lg_b * chunk)

    def kv_outer(i, kdec):
        ks = (k_ref[0, i * chunk:(i + 1) * chunk, :].astype(F32) * kdec).astype(BF16)
        vi = v_ref[0, i * chunk:(i + 1) * chunk, :]
        return lax.dot_general(ks, vi, (((0,), (0,)), ((), ())), preferred_element_type=F32)

    state = jnp.zeros((RET_QK_DIM, RET_V_DIM), F32)
    for i in range(n_chunks - 1, -1, -1):
        tb_ref[i] = state.astype(BF16)
        if i > 0:
            state = state * cdec_b + kv_outer(i, kdec_b)

    state = jnp.zeros((RET_QK_DIM, RET_V_DIM), F32)
    for i in range(n_chunks):
        sl = slice(i * chunk, (i + 1) * chunk)
        qi = q_ref[0, sl, :]
        ki = k_ref[0, sl, :]
        vi = v_ref[0, sl, :]
        s = lax.dot_general(qi, ki, (((1,), (1,)), ((), ())), preferred_element_type=F32)
        y = jnp.dot((s * dmat).astype(BF16), vi, preferred_element_type=F32)
        y = y + jnp.dot(qi, tb_ref[i], preferred_element_type=F32) * qdec_b
        if i > 0:
            y = y + jnp.dot(qi, state.astype(BF16), preferred_element_type=F32) * qdec_f
        if i < n_chunks - 1:
            state = state * cdec_f + kv_outer(i, kdec_f)
        y = y * lax.rsqrt(jnp.mean(y * y, axis=-1, keepdims=True) + GN_EPS)
        o_ref[0, sl, :] = (g_ref[0, sl, :].astype(F32) * y).astype(o_ref.dtype)


def retention(qk, v, g, lg):
    b, s, _ = qk.shape
    nh = RET_HEADS
    dk, dv = RET_QK_DIM, RET_V_DIM
    chunk = min(RET_CHUNK, s)
    assert s % chunk == 0
    return pl.pallas_call(
        functools.partial(_retention_kernel, seq=s, chunk=chunk),
        grid=(b, nh),
        in_specs=[pl.BlockSpec(memory_space=pltpu.SMEM),
                  pl.BlockSpec((1, s, dk), lambda i, h: (i, 0, h)),
                  pl.BlockSpec((1, s, dk), lambda i, h: (i, 0, nh + h)),
                  pl.BlockSpec((1, s, dv), lambda i, h: (i, 0, h)),
                  pl.BlockSpec((1, s, dv), lambda i, h: (i, 0, h))],
        out_specs=pl.BlockSpec((1, s, dv), lambda i, h: (i, 0, h)),
        out_shape=jax.ShapeDtypeStruct((b, s, nh * dv), BF16),
        scratch_shapes=[pltpu.VMEM((s // chunk, dk, dv), BF16)],
        compiler_params=_params(("parallel", "arbitrary")),
        name="retention",
    )(lg, qk, qk, v, g)


def _route_rows(x, whl_ref, b_ref, idx_ref, wts_ref, rows):
    xh = x.astype(BF16)
    xl = (x - xh.astype(F32)).astype(BF16)
    both = jnp.dot(xh, whl_ref[...], preferred_element_type=F32)
    logits = (both[:, :ROUTER_LANES] + both[:, ROUTER_LANES:]
              + jnp.dot(xl, whl_ref[:, :ROUTER_LANES], preferred_element_type=F32)) + b_ref[...]
    lane = lax.broadcasted_iota(jnp.int32, logits.shape, 1)
    neg = -jnp.inf
    big = jnp.int32(ROUTER_LANES)

    gl = jnp.where(lane < N_GROUPS, logits, neg)
    gmax = jnp.max(gl, axis=-1, keepdims=True)
    g_sel = jnp.min(jnp.where(gl == gmax, lane, big), axis=-1, keepdims=True)
    w_grp = 1.0 / jnp.sum(jnp.exp(gl - gmax), axis=-1, keepdims=True)

    lo = N_GROUPS + EXPERTS_PER_GROUP * g_sel
    el = jnp.where((lane >= lo) & (lane < lo + EXPERTS_PER_GROUP), logits, neg)
    v1 = jnp.max(el, axis=-1, keepdims=True)
    i1 = jnp.min(jnp.where(el == v1, lane, big), axis=-1, keepdims=True)
    el2 = jnp.where(lane == i1, neg, el)
    v2 = jnp.max(el2, axis=-1, keepdims=True)
    i2 = jnp.min(jnp.where(el2 == v2, lane, big), axis=-1, keepdims=True)
    e21 = jnp.exp(v2 - v1)
    p1 = 1.0 / (1.0 + e21)
    p2 = e21 * p1
    idx_ref[rows, 0:1] = i1 - N_GROUPS
    idx_ref[rows, 1:2] = i2 - N_GROUPS
    wts_ref[rows, 0:1] = p1 * w_grp
    wts_ref[rows, 1:2] = p2 * w_grp


def _router_operands(w_group, b_group, w_expert, b_expert):
    d = w_group.shape[0]
    w = jnp.concatenate([w_group.astype(F32), w_expert.astype(F32).reshape(d, N_EXPERTS)], axis=1)
    w = jnp.pad(w, ((0, 0), (0, ROUTER_LANES - w.shape[1])))
    wh = w.astype(BF16)
    wl = (w - wh.astype(F32)).astype(BF16)
    bias = jnp.concatenate([b_group.astype(F32), b_expert.astype(F32).reshape(N_EXPERTS)])
    bias = jnp.pad(bias, (0, ROUTER_LANES - bias.shape[0])).reshape(1, ROUTER_LANES)
    return jnp.concatenate([wh, wl], axis=1), bias


def _dispatch_kernel(pos_ref, tile_fill_ref, x_ref, xs_hbm, sem, fill_sem, *, tm):
    t = pl.program_id(0)

    @pl.when(t == 0)
    def _():
        def tile_body(j, n_started):
            @pl.when(tile_fill_ref[j] > 0)
            def _():
                dst = xs_hbm.at[pl.ds(pl.multiple_of(j * tm, tm), tm), :]
                pltpu.make_async_copy(x_ref, dst, fill_sem).start()
            return n_started + tile_fill_ref[j]
        n_fills = lax.fori_loop(0, tile_fill_ref.shape[0], tile_body, 0)

        def wait_fill(i, carry):
            pltpu.make_async_copy(x_ref, xs_hbm.at[pl.ds(0, tm), :], fill_sem).wait()
            return carry
        lax.fori_loop(0, n_fills, wait_fill, 0)

    for i in range(tm):
        for k in range(TOP_K):
            dst = xs_hbm.at[pl.ds(pos_ref[(t * tm + i) * TOP_K + k], 1), :]
            pltpu.make_async_copy(x_ref.at[pl.ds(i, 1), :], dst, sem).start()
    for _ in range(TOP_K):
        pltpu.make_async_copy(x_ref, xs_hbm.at[pl.ds(0, tm), :], sem).wait()


def dispatch_rows(x, pos, tile_fill, tm):
    n, d = x.shape
    assert n % tm == 0
    grid_spec = pltpu.PrefetchScalarGridSpec(
        num_scalar_prefetch=2,
        grid=(n // tm,),
        in_specs=[pl.BlockSpec((tm, d), lambda t, *_: (t, 0))],
        out_specs=pl.BlockSpec(memory_space=pl.ANY),
        scratch_shapes=[pltpu.SemaphoreType.DMA] * 2,
    )
    return pl.pallas_call(
        functools.partial(_dispatch_kernel, tm=tm),
        grid_spec=grid_spec,
        out_shape=jax.ShapeDtypeStruct((tile_fill.shape[0] * tm, d), x.dtype),
        compiler_params=_params(("arbitrary",)),
        name="dispatch_rows",
    )(pos, tile_fill, x)


def _routing_metadata(expert_idx, tm):
    n = expert_idx.shape[0]
    na = n * TOP_K
    n_tiles = na // tm + N_EXPERTS
    e_flat = expert_idx.reshape(na)
    onehot = (e_flat[:, None] == jnp.arange(N_EXPERTS, dtype=jnp.int32)[None, :]).astype(F32)
    blk = min(METADATA_BLOCK, na)
    oh_blocks = onehot.reshape(na // blk, blk, N_EXPERTS)
    within = jnp.einsum("ij,bje->bie", jnp.tril(jnp.ones((blk, blk), F32)), oh_blocks,
                        precision=lax.Precision.HIGHEST)
    block_total = within[:, -1, :]
    block_offset = jnp.cumsum(block_total, axis=0) - block_total
    csum = (within + block_offset[:, None, :]).reshape(na, N_EXPERTS)
    counts = (block_offset[-1] + block_total[-1]).astype(jnp.int32)
    rank = jnp.sum(onehot * csum, axis=1).astype(jnp.int32) - 1
    padded = ((counts + tm - 1) // tm) * tm
    ends = jnp.cumsum(padded)
    starts = ends - padded
    pos = (starts[e_flat] + rank).astype(jnp.int32)
    tile_start = jnp.arange(n_tiles, dtype=jnp.int32) * tm
    tile_expert = jnp.sum((tile_start[:, None] >= ends[None, :]).astype(jnp.int32), axis=1)
    tile_valid = (tile_start < ends[-1]).astype(jnp.int32)
    last_expert = jnp.max(jnp.where(counts > 0, jnp.arange(N_EXPERTS, dtype=jnp.int32), 0))
    tile_expert = jnp.where(tile_valid > 0, tile_expert, last_expert)
    is_last_of_expert = jnp.any((tile_start[:, None] + tm == ends[None, :]) & (counts > 0)[None, :], axis=1)
    tile_fill = (is_last_of_expert | (tile_valid == 0)).astype(jnp.int32)
    return pos, tile_fill, tile_expert, tile_valid


def _row_gather_start(src_hbm, row_of, base, dst, sem, n_rows, *, unrolled=False):
    def start(r):
        pltpu.make_async_copy(src_hbm.at[pl.ds(row_of(base + r), 1), :], dst.at[pl.ds(r, 1), :], sem).start()

    if unrolled:
        for r in range(n_rows):
            start(r)
    else:
        def body(r, carry):
            start(r)
            return carry
        lax.fori_loop(0, n_rows, body, 0, unroll=8)


def _row_gather_wait(src_hbm, dst, sem, n_rows):
    pltpu.make_async_copy(src_hbm.at[pl.ds(0, n_rows), :], dst, sem).wait()


def _grouped_swiglu_kernel(tile_expert_ref, tile_valid_ref, x_ref, wg_ref, wu_ref, wd_ref, o_ref, wgb, wub, wdb):
    t = pl.program_id(0)
    prev = jnp.maximum(t - 1, 0)
    valid = tile_valid_ref[t] > 0

    @pl.when(valid)
    def _():
        @pl.when((t == 0) | (tile_expert_ref[t] != tile_expert_ref[prev]))
        def _():
            wgb[...] = wg_ref[0].astype(BF16)
            wub[...] = wu_ref[0].astype(BF16)
            wdb[...] = wd_ref[0].astype(BF16)

        xb = x_ref[...].astype(BF16)
        hg = jnp.dot(xb, wgb[...], preferred_element_type=F32)
        hu = jnp.dot(xb, wub[...], preferred_element_type=F32)
        hcur = (hg / (1.0 + jnp.exp(-hg)) * hu).astype(BF16)
        o_ref[...] = jnp.dot(hcur, wdb[...], preferred_element_type=F32)

    @pl.when(jnp.logical_not(valid))
    def _():
        o_ref[...] = jnp.zeros_like(o_ref)


def grouped_swiglu(x_sorted, tile_expert, tile_valid, w_gate, w_up, w_down, *, tm=MOE_TM):
    d, f = w_gate.shape[-2:]
    n_tiles = tile_expert.shape[0]
    grid_spec = pltpu.PrefetchScalarGridSpec(
        num_scalar_prefetch=2,
        grid=(n_tiles,),
        in_specs=[pl.BlockSpec((tm, d), lambda t, te, tv: (t, 0)),
                  pl.BlockSpec((1, d, f), lambda t, te, tv: (te[t], 0, 0)),
                  pl.BlockSpec((1, d, f), lambda t, te, tv: (te[t], 0, 0)),
                  pl.BlockSpec((1, f, d), lambda t, te, tv: (te[t], 0, 0))],
        out_specs=pl.BlockSpec((tm, d), lambda t, te, tv: (t, 0)),
        scratch_shapes=[pltpu.VMEM((d, f), BF16), pltpu.VMEM((d, f), BF16), pltpu.VMEM((f, d), BF16)],
    )
    return pl.pallas_call(
        _grouped_swiglu_kernel,
        grid_spec=grid_spec,
        out_shape=jax.ShapeDtypeStruct((n_tiles * tm, d), F32),
        compiler_params=_params(("arbitrary",)),
        name="grouped_swiglu",
    )(tile_expert, tile_valid, x_sorted, w_gate, w_up, w_down)


def _combine_ln_kernel(pos_ref, rows_hbm, x_ref, wts_ref, g_ref, b_ref, *refs, tm):
    out_refs, (buf0, buf1, sem) = refs[:-3], refs[-3:]
    t = pl.program_id(0)
    slot = lax.rem(t, 2)

    def start(tile, sl, unrolled):
        _row_gather_start(rows_hbm, lambda a: pos_ref[2 * a], tile * tm, buf0.at[sl], sem.at[0, sl], tm,
                          unrolled=unrolled)
        _row_gather_start(rows_hbm, lambda a: pos_ref[2 * a + 1], tile * tm, buf1.at[sl], sem.at[1, sl], tm,
                          unrolled=unrolled)

    @pl.when(t == 0)
    def _():
        start(0, 0, False)

    _row_gather_wait(rows_hbm, buf0.at[slot], sem.at[0, slot], tm)
    _row_gather_wait(rows_hbm, buf1.at[slot], sem.at[1, slot], tm)

    @pl.when(t + 1 < pl.num_programs(0))
    def _():
        start(t + 1, 1 - slot, True)

    w = wts_ref[...]
    z = DEEPNORM_ALPHA * x_ref[...] + w[:, 0:1] * buf0[slot] + w[:, 1:2] * buf1[slot]
    y = _layer_norm_rows(z, g_ref[...], b_ref[...])
    for o_ref in out_refs:
        o_ref[...] = y.astype(o_ref.dtype)


def combine_residual_layernorm(rows, pos, x, wts, gain, bias, *, out_dtypes, tm=COMBINE_TM):
    n, d = x.shape
    tm = min(tm, n)
    grid_spec = pltpu.PrefetchScalarGridSpec(
        num_scalar_prefetch=1,
        grid=(n // tm,),
        in_specs=[pl.BlockSpec(memory_space=pl.ANY),
                  pl.BlockSpec((tm, d), lambda t, p: (t, 0)),
                  pl.BlockSpec((tm, TOP_K), lambda t, p: (t, 0)),
                  pl.BlockSpec((1, d), lambda t, p: (0, 0)),
                  pl.BlockSpec((1, d), lambda t, p: (0, 0))],
        out_specs=[pl.BlockSpec((tm, d), lambda t, p: (t, 0)) for _ in out_dtypes],
        scratch_shapes=[pltpu.VMEM((2, tm, d), F32), pltpu.VMEM((2, tm, d), F32),
                        pltpu.SemaphoreType.DMA((2, 2))],
    )
    return pl.pallas_call(
        functools.partial(_combine_ln_kernel, tm=tm),
        grid_spec=grid_spec,
        out_shape=[jax.ShapeDtypeStruct((n, d), dt) for dt in out_dtypes],
        compiler_params=_params(("arbitrary",)),
        name="moe_combine_ln",
    )(pos, rows, x, wts, gain.reshape(1, d), bias.reshape(1, d))


def moe_block(x, expert_idx, wts, w_gate, w_up, w_down, layer, gain, bias, *, out_dtypes, tm=MOE_TM):
    d, f = w_gate.shape[-2:]
    pos, tile_fill, tile_expert, tile_valid = _routing_metadata(expert_idx, tm)
    x_sorted = dispatch_rows(x, pos, tile_fill, tm)
    rows = grouped_swiglu(x_sorted, tile_expert + layer * N_EXPERTS, tile_valid,
                          w_gate.reshape(-1, d, f), w_up.reshape(-1, d, f), w_down.reshape(-1, f, d), tm=tm)
    return combine_residual_layernorm(rows, pos, x, wts, gain, bias, out_dtypes=out_dtypes)


def _rotary_tables(seq):
    half = RET_QK_DIM // 2
    freqs = ROPE_BASE ** (-jnp.arange(half, dtype=F32) / half)
    ang = jnp.arange(seq).astype(F32)[:, None] * freqs[None, :]
    return jnp.cos(ang), jnp.sin(ang)


def kernel(x, attn_w_in, attn_w_out, attn_rpb, ret_w_in, ret_w_out, ret_decay_fwd, ret_decay_bwd,
           moe_w_group_router, moe_b_group_router, moe_w_expert_router, moe_b_expert_router,
           moe_w_gate, moe_w_up, moe_w_down, ln_gain, ln_bias):
    b, s, d = x.shape
    n = b * s
    xf = x.reshape(n, d).astype(F32)
    xb = xf
    depth = ln_gain.shape[0]
    for i in range(depth):
        j = i // 2
        if i % 2 == 0:
            qkv = matmul(xb, attn_w_in[j].astype(F32))
            bias_tables = _attention_bias_tables(attn_rpb[j], s // GRID_W)
            mixed = neighbourhood_attention(qkv.reshape(b, s, 3 * d), bias_tables).reshape(n, d)
            w_out = attn_w_out[j]
        else:
            w_in = ret_w_in[j].astype(F32)
            qk_w, v_w = 2 * RET_HEADS * RET_QK_DIM, RET_HEADS * RET_V_DIM
            qk = matmul(xb, w_in, col_start=0, n_cols=qk_w, epilogue="rotary", rotary=_rotary_tables(s),
                        scale_from_col=qk_w // 2)
            v = matmul(xb, w_in, col_start=qk_w, n_cols=v_w)
            g = matmul(xb, w_in, col_start=qk_w + v_w, n_cols=v_w, epilogue="silu")
            lg = jnp.stack([jnp.log1p(-jnp.exp(ret_decay_fwd[j].astype(F32))),
                            jnp.log1p(-jnp.exp(ret_decay_bwd[j].astype(F32)))])
            mixed = retention(qk.reshape(b, s, qk_w), v.reshape(b, s, v_w), g.reshape(b, s, v_w),
                              lg).reshape(n, v_w)
            w_out = ret_w_out[j]
        whl, rbias = _router_operands(moe_w_group_router[i], moe_b_group_router[i], moe_w_expert_router[i],
                                      moe_b_expert_router[i])
        xf, expert_idx, wts = matmul_residual_layernorm_route(mixed, w_out.astype(BF16), xf, ln_gain[i, 0],
                                                              ln_bias[i, 0], whl, rbias)
        last = i == depth - 1
        outs = moe_block(xf, expert_idx, wts, moe_w_gate, moe_w_up, moe_w_down, i,
                         ln_gain[i, 1], ln_bias[i, 1], out_dtypes=(F32,) if last else (F32, BF16))
        xf, xb = outs[0], outs[-1]
    return xf.reshape(b, s, d).astype(x.dtype)
```
